```python
import jax
import jax.numpy as jnp
from jax import lax
import numpy as np

D_MODEL = 1024
BATCH = 2
SEQ = 8192
DEPTH = 1
DEC_BATCH = 128
DEC_SEQ = 4
PAST_LEN = 2048
PAGE_SIZE = 128

M_HEADS = 4
M_DQK = D_MODEL // 8
M_DV = D_MODEL // 4
CONV_W = 4
M_CHUNK = 64
ATTN_GROUPS = ((128, 1), (512, 4), (2048, 16))
A_HEADS = 4
A_DH = D_MODEL // 16
N_EXPERTS = 32
TOP_K = 4
D_FF = D_MODEL
SWIGLU_LIMIT = 7.0
SWIGLU_ALPHA = 1.702
MOE_BLOCK = 128
DN_ALPHA = (2 * DEPTH) ** 0.25
DN_BETA = (8 * DEPTH) ** -0.25
LN_EPS = 1e-5

QK_W = 2 * M_HEADS * M_DQK
MV_W = M_HEADS * M_DV
A_W = A_HEADS * A_DH
OFF_V = QK_W
OFF_I = OFF_V + MV_W
OFF_F = OFF_I + M_HEADS
OFF_O = OFF_F + M_HEADS
OFF_A = OFF_O + MV_W
P_W = OFF_A + 3 * A_W * len(ATTN_GROUPS)

kernel_name = 'hybrid_mlstm_dilated_attn_moe_step'


def layer_norm(x, g=None, b=None):
    xf = x.astype(jnp.float32)
    mu = jnp.mean(xf, axis=-1, keepdims=True)
    var = jnp.mean(jnp.square(xf - mu), axis=-1, keepdims=True)
    y = (xf - mu) * lax.rsqrt(var + LN_EPS)
    if g is not None:
        y = y * g.astype(jnp.float32)
    if b is not None:
        y = y + b.astype(jnp.float32)
    return y.astype(x.dtype)


def mlstm_chunk_step(carry, xs):
    C, n, m = carry
    q, k, v, li, lf = xs
    L = q.shape[2]
    b = jnp.cumsum(lf, axis=-1)
    causal = jnp.tril(jnp.ones((L, L), dtype=bool))
    dmat = jnp.where(causal, b[..., :, None] - b[..., None, :] + li[..., None, :], -jnp.inf)
    inter = b + m[..., None]
    m_t = jnp.maximum(inter, jnp.max(dmat, axis=-1))
    a = jnp.exp(dmat - m_t[..., None]) * jnp.einsum('bhtk,bhsk->bhts', q, k)
    sc = jnp.exp(inter - m_t)
    num = sc[..., None] * jnp.einsum('bhtk,bhkv->bhtv', q, C) + jnp.einsum('bhts,bhsv->bhtv', a, v)
    den = sc * jnp.einsum('bhtk,bhk->bht', q, n) + jnp.sum(a, axis=-1)
    h = num / jnp.maximum(jnp.abs(den), jnp.exp(-m_t))[..., None]
    b_end = b[..., -1]
    g = b_end[..., None] - b + li
    m_new = jnp.maximum(b_end + m, jnp.max(g, axis=-1))
    w = jnp.exp(g - m_new[..., None])
    decay = jnp.exp(b_end + m - m_new)
    C_new = decay[..., None, None] * C + jnp.einsum('bhs,bhsk,bhsv->bhkv', w, k, v)
    n_new = decay[..., None] * n + jnp.einsum('bhs,bhsk->bhk', w, k)
    return (C_new, n_new, m_new), h


def mlstm_chunkwise(q, k, v, li, lf, C0, n0, m0):
    B, H, T, _ = q.shape
    L = min(M_CHUNK, T)
    Tp = -(-T // L) * L
    pad = Tp - T
    if pad:
        p3 = ((0, 0), (0, 0), (0, pad))
        q = jnp.pad(q, p3 + ((0, 0),))
        k = jnp.pad(k, p3 + ((0, 0),))
        v = jnp.pad(v, p3 + ((0, 0),))
        li = jnp.pad(li, p3, constant_values=-jnp.inf)
        lf = jnp.pad(lf, p3)
    nC = Tp // L

    def chunks(a):
        return jnp.moveaxis(a.reshape((B, H, nC, L) + a.shape[3:]), 2, 0)

    (C, n, m), h = lax.scan(mlstm_chunk_step, (C0, n0, m0),
                            (chunks(q), chunks(k), chunks(v), chunks(li), chunks(lf)))
    h = jnp.moveaxis(h, 0, 2).reshape(B, H, Tp, -1)[:, :, :T]
    return h, C, n, m


def masked_softmax_stats(s, mask):
    s = jnp.where(mask, s, -jnp.inf)
    mx = jnp.max(s, axis=-1, keepdims=True)
    p = jnp.exp(s - mx)
    l = jnp.sum(p, axis=-1, keepdims=True)
    return p / l, (mx + jnp.log(l))[..., 0]


def dilated_attn_prompt(q, k, v, window, dilation):
    B, T, H, Dh = q.shape
    blk = window // dilation
    span = dilation * blk
    Tp = -(-T // span) * span
    Mp = Tp // dilation
    nb = Mp // blk

    def to_blocks(a):
        a = jnp.pad(a, ((0, 0), (0, Tp - T), (0, 0), (0, 0)))
        a = a.reshape(B, Mp, dilation, H, Dh).transpose(0, 2, 1, 3, 4)
        return a.reshape(B, dilation, nb, blk, H, Dh)

    def with_prev(a):
        prev = jnp.pad(a[:, :, :-1], ((0, 0), (0, 0), (1, 0), (0, 0), (0, 0), (0, 0)))
        return jnp.concatenate([prev, a], axis=3)

    qb = to_blocks(q)
    kk = with_prev(to_blocks(k))
    vv = with_prev(to_blocks(v))
    qi = jnp.arange(blk)[:, None]
    ki = jnp.arange(2 * blk)[None, :]
    delta = qi + blk - ki
    band = (delta >= 0) & (delta <= blk)
    first = (jnp.arange(nb)[:, None, None] > 0) | (ki >= blk)[None]
    mask = (band[None] & first)[None, None, :, None]
    s = jnp.einsum('bdnqhe,bdnkhe->bdnhqk', qb, kk).astype(jnp.float32) * (Dh ** -0.5)
    p, lse = masked_softmax_stats(s, mask)
    o = jnp.einsum('bdnhqk,bdnkhe->bdnqhe', p, vv.astype(jnp.float32))
    o = o.reshape(B, dilation, Mp, H, Dh).transpose(0, 2, 1, 3, 4).reshape(B, Tp, H, Dh)[:, :T]
    lse = lse.transpose(0, 1, 2, 4, 3).reshape(B, dilation, Mp, H).transpose(0, 2, 1, 3).reshape(B, Tp, H)[:, :T]
    return o, lse


def dilated_attn_sample(q, kv_new, kv_cache, window, dilation):
    L = kv_cache.shape[1]
    S = q.shape[1]
    Dh = q.shape[-1]
    nk = window // dilation
    kv_all = jnp.concatenate([kv_cache, kv_new], axis=1)
    idx = L + jnp.arange(S)[:, None] - dilation * jnp.arange(nk + 1)[None, :]
    valid = idx >= 0
    g = kv_all[:, jnp.maximum(idx, 0)]
    s = jnp.einsum('bshe,bsjhe->bshj', q, g[:, :, :, 0]).astype(jnp.float32) * (Dh ** -0.5)
    p, lse = masked_softmax_stats(s, valid[None, :, None, :])
    o = jnp.einsum('bshj,bsjhe->bshe', p, g[:, :, :, 1].astype(jnp.float32))
    return o, lse


def token_mixer(h, conv_buf, C0, n0, m0, kv_caches, p):
    B, T, _ = h.shape
    f32 = jnp.float32
    proj = h @ p['w_in'] + p['b_in']
    qk_pre = proj[..., :OFF_V]
    v_m = proj[..., OFF_V:OFF_I].reshape(B, T, M_HEADS, M_DV)
    i_pre = proj[..., OFF_I:OFF_F]
    f_pre = proj[..., OFF_F:OFF_O]
    o_pre = proj[..., OFF_O:OFF_A]
    xc = jnp.concatenate([conv_buf.astype(h.dtype), qk_pre], axis=1)
    conv = p['conv_b'] + p['conv_w'][0] * xc[:, 0:T]
    for j in range(1, CONV_W):
        conv = conv + p['conv_w'][j] * xc[:, j:j + T]
    new_conv = xc[:, -(CONV_W - 1):]
    act = jax.nn.silu(conv)
    q = act[..., :QK_W // 2].reshape(B, T, M_HEADS, M_DQK)
    k = act[..., QK_W // 2:].reshape(B, T, M_HEADS, M_DQK) * (M_DQK ** -0.5)
    tr = lambda t: jnp.swapaxes(t, 1, 2).astype(f32)
    hm, C, n, m = mlstm_chunkwise(tr(q), tr(k), tr(v_m), tr(i_pre),
                                  tr(jax.nn.log_sigmoid(f_pre.astype(f32))),
                                  C0.astype(f32), n0.astype(f32), m0.astype(f32))
    hm = layer_norm(hm, p['mh_norm_w'][None, :, None, :])
    hm = jnp.swapaxes(hm, 1, 2).reshape(B, T, MV_W).astype(h.dtype) * jax.nn.sigmoid(o_pre)
    br_m = hm @ p['w_br_mlstm']
    outs, lses, new_kv = [], [], []
    for gi, (window, dil) in enumerate(ATTN_GROUPS):
        base = OFF_A + 3 * A_W * gi
        qg = proj[..., base:base + A_W].reshape(B, T, A_HEADS, A_DH)
        kvg = proj[..., base + A_W:base + 3 * A_W].reshape(B, T, 2, A_HEADS, A_DH)
        if kv_caches is None:
            o, lse = dilated_attn_prompt(qg, kvg[:, :, 0], kvg[:, :, 1], window, dil)
            new_kv.append(kvg[:, T - min(window, T):])
        else:
            o, lse = dilated_attn_sample(qg, kvg, kv_caches[gi].astype(h.dtype), window, dil)
            new_kv.append(kvg)
        outs.append(o)
        lses.append(lse)
    wg = jax.nn.softmax(jnp.stack(lses, 0), axis=0)
    attn = jnp.einsum('gbth,gbthe->bthe', wg, jnp.stack(outs, 0)).reshape(B, T, A_W).astype(h.dtype)
    br_a = attn @ p['w_br_attn']
    gates = jax.nn.sigmoid(h @ p['w_merge'] + p['b_merge'])
    merged = gates[..., :D_MODEL] * br_m + gates[..., D_MODEL:] * br_a
    out = merged @ p['w_o']
    return out, (new_kv[0], new_kv[1], new_kv[2], C, n, m, new_conv)


def moe_ffn(h, w_router, b_router, w_up, b_up, w_down, b_down):
    N, D = h.shape
    logits = (h @ w_router + b_router).astype(jnp.float32)
    top_v, top_i = lax.top_k(logits, TOP_K)
    gate = jax.nn.softmax(top_v, axis=-1)
    A = N * TOP_K
    flat_e = top_i.reshape(A)
    flat_tok = jnp.repeat(jnp.arange(N, dtype=jnp.int32), TOP_K)
    flat_gate = gate.reshape(A)
    order = jnp.argsort(flat_e)
    sorted_e = flat_e[order]
    counts = jnp.bincount(flat_e, length=N_EXPERTS)
    padded = (counts + MOE_BLOCK - 1) // MOE_BLOCK * MOE_BLOCK
    starts = jnp.cumsum(counts) - counts
    pend = jnp.cumsum(padded)
    pstart = pend - padded
    dest = pstart[sorted_e] + (jnp.arange(A) - starts[sorted_e])
    n_rows = -(-A // MOE_BLOCK) * MOE_BLOCK + N_EXPERTS * MOE_BLOCK
    nblk = n_rows // MOE_BLOCK
    row_tok = jnp.zeros((n_rows,), jnp.int32).at[dest].set(flat_tok[order])
    row_gate = jnp.zeros((n_rows,), jnp.float32).at[dest].set(flat_gate[order])
    blk_e = jnp.clip(jnp.searchsorted(pend, jnp.arange(nblk) * MOE_BLOCK, side='right'), 0, N_EXPERTS - 1)
    xs = h[row_tok].reshape(nblk, MOE_BLOCK, D)

    def expert_block(args):
        xb, e = args
        hu = xb @ w_up[e] + b_up[e]
        x_glu = jnp.minimum(hu[:, :D_FF], SWIGLU_LIMIT)
        x_lin = jnp.clip(hu[:, D_FF:], -SWIGLU_LIMIT, SWIGLU_LIMIT)
        act = x_glu * jax.nn.sigmoid(SWIGLU_ALPHA * x_glu) * (x_lin + 1)
        return act @ w_down[e] + b_down[e]

    ys = lax.map(expert_block, (xs, blk_e)).reshape(n_rows, D)
    out = jnp.zeros((N, D), jnp.float32).at[row_tok].add(ys.astype(jnp.float32) * row_gate[:, None])
    return out.astype(h.dtype)


def decoder_layer(x, c, conv_buf, C0, n0, m0, kv_caches, p):
    mod = jax.nn.silu(c) @ p['w_ada'] + p['b_ada']
    sh1, sc1, g1, sh2, sc2, g2 = jnp.split(mod[:, None, :], 6, axis=-1)
    h = layer_norm(x) * (1 + sc1) + sh1
    mix, states = token_mixer(h, conv_buf, C0, n0, m0, kv_caches, p)
    x = layer_norm(DN_ALPHA * x + g1 * mix, p['ln1_g'], p['ln1_b'])
    h = layer_norm(x) * (1 + sc2) + sh2
    B, T, D = h.shape
    f = moe_ffn(h.reshape(B * T, D), p['w_router'], p['b_router'], p['w_up'], p['b_up'],
                p['w_down'], p['b_down']).reshape(B, T, D)
    x = layer_norm(DN_ALPHA * x + g2 * f, p['ln2_g'], p['ln2_b'])
    return x, states


def setup_inputs(seed: int = 0) -> dict:
    key = jax.random.key(seed)
    ks = iter(jax.random.split(key, 48))
    f32 = jnp.float32

    def nrm(shape, s=1.0):
        return s * jax.random.normal(next(ks), shape, f32)

    win = [min(w, PAST_LEN) for w, _ in ATTN_GROUPS]
    b_in = nrm((DEPTH, P_W), 0.02).at[:, OFF_F:OFF_O].set(
        jax.random.uniform(next(ks), (DEPTH, M_HEADS), f32, 3.0, 6.0))
    return {
        'x_prompt': nrm((BATCH, SEQ, D_MODEL)),
        'x_sample': nrm((DEC_BATCH, DEC_SEQ, D_MODEL)),
        'c_prompt': nrm((BATCH, D_MODEL)),
        'c_sample': nrm((DEC_BATCH, D_MODEL)),
        'cache_kv_w128': nrm((DEPTH, DEC_BATCH, win[0], 2, A_HEADS, A_DH)),
        'cache_kv_w512': nrm((DEPTH, DEC_BATCH, win[1], 2, A_HEADS, A_DH)),
        'cache_kv_w2048': nrm((DEPTH, DEC_BATCH, win[2], 2, A_HEADS, A_DH)),
        'state_mlstm_C': nrm((DEPTH, DEC_BATCH, M_HEADS, M_DQK, M_DV), 0.5),
        'state_mlstm_n': nrm((DEPTH, DEC_BATCH, M_HEADS, M_DQK), 0.5),
        'state_mlstm_m': jax.random.uniform(next(ks), (DEPTH, DEC_BATCH, M_HEADS), f32, 0.0, 2.0),
        'state_qk_conv': nrm((DEPTH, DEC_BATCH, CONV_W - 1, QK_W)),
        'w_ada': nrm((DEPTH, D_MODEL, 6 * D_MODEL), 0.5 * D_MODEL ** -0.5),
        'b_ada': nrm((DEPTH, 6 * D_MODEL), 0.1),
        'w_in': nrm((DEPTH, D_MODEL, P_W), D_MODEL ** -0.5),
        'b_in': b_in,
        'conv_w': nrm((DEPTH, CONV_W, QK_W), CONV_W ** -0.5),
        'conv_b': nrm((DEPTH, QK_W), 0.02),
        'mh_norm_w': 1.0 + nrm((DEPTH, M_HEADS, M_DV), 0.02),
        'w_br_mlstm': nrm((DEPTH, MV_W, D_MODEL), MV_W ** -0.5),
        'w_br_attn': nrm((DEPTH, A_W, D_MODEL), A_W ** -0.5),
        'w_merge': nrm((DEPTH, D_MODEL, 2 * D_MODEL), D_MODEL ** -0.5),
        'b_merge': nrm((DEPTH, 2 * D_MODEL), 0.02),
        'w_o': nrm((DEPTH, D_MODEL, D_MODEL), DN_BETA * D_MODEL ** -0.5),
        'ln1_g': 1.0 + nrm((DEPTH, D_MODEL), 0.02),
        'ln1_b': nrm((DEPTH, D_MODEL), 0.02),
        'w_router': nrm((DEPTH, D_MODEL, N_EXPERTS), D_MODEL ** -0.5),
        'b_router': nrm((DEPTH, N_EXPERTS), 0.01),
        'w_up': nrm((DEPTH, N_EXPERTS, D_MODEL, 2 * D_FF), D_MODEL ** -0.5),
        'b_up': nrm((DEPTH, N_EXPERTS, 2 * D_FF), 0.02),
        'w_down': nrm((DEPTH, N_EXPERTS, D_FF, D_MODEL), DN_BETA * D_FF ** -0.5),
        'b_down': nrm((DEPTH, N_EXPERTS, D_MODEL), 0.02),
        'ln2_g': 1.0 + nrm((DEPTH, D_MODEL), 0.02),
        'ln2_b': nrm((DEPTH, D_MODEL), 0.02),
    }


def reference(x_prompt, x_sample, c_prompt, c_sample, cache_kv_w128, cache_kv_w512, cache_kv_w2048,
              state_mlstm_C, state_mlstm_n, state_mlstm_m, state_qk_conv,
              w_ada, b_ada, w_in, b_in, conv_w, conv_b, mh_norm_w, w_br_mlstm, w_br_attn,
              w_merge, b_merge, w_o, ln1_g, ln1_b, w_router, b_router, w_up, b_up, w_down, b_down,
              ln2_g, ln2_b):
    xp, xs = x_prompt, x_sample
    Bp = x_prompt.shape[0]
    new_p = [[] for _ in range(7)]
    new_s = [[] for _ in range(7)]
    for layer in range(DEPTH):
        p = {'w_ada': w_ada[layer], 'b_ada': b_ada[layer], 'w_in': w_in[layer], 'b_in': b_in[layer],
             'conv_w': conv_w[layer], 'conv_b': conv_b[layer], 'mh_norm_w': mh_norm_w[layer],
             'w_br_mlstm': w_br_mlstm[layer], 'w_br_attn': w_br_attn[layer],
             'w_merge': w_merge[layer], 'b_merge': b_merge[layer], 'w_o': w_o[layer],
             'ln1_g': ln1_g[layer], 'ln1_b': ln1_b[layer], 'w_router': w_router[layer],
             'b_router': b_router[layer], 'w_up': w_up[layer], 'b_up': b_up[layer],
             'w_down': w_down[layer], 'b_down': b_down[layer], 'ln2_g': ln2_g[layer], 'ln2_b': ln2_b[layer]}
        xp, sp = decoder_layer(
            xp, c_prompt,
            jnp.zeros((Bp, CONV_W - 1, QK_W), xp.dtype),
            jnp.zeros((Bp, M_HEADS, M_DQK, M_DV), jnp.float32),
            jnp.zeros((Bp, M_HEADS, M_DQK), jnp.float32),
            jnp.zeros((Bp, M_HEADS), jnp.float32),
            None, p)
        xs, ss = decoder_layer(
            xs, c_sample, state_qk_conv[layer], state_mlstm_C[layer], state_mlstm_n[layer],
            state_mlstm_m[layer], (cache_kv_w128[layer], cache_kv_w512[layer], cache_kv_w2048[layer]), p)
        for lst, a in zip(new_p, sp):
            lst.append(a)
        for lst, a in zip(new_s, ss):
            lst.append(a)
    return (xp, xs,
            jnp.stack(new_p[0]), jnp.stack(new_p[1]), jnp.stack(new_p[2]),
            jnp.stack(new_p[3]), jnp.stack(new_p[4]), jnp.stack(new_p[5]), jnp.stack(new_p[6]),
            jnp.stack(new_s[0]), jnp.stack(new_s[1]), jnp.stack(new_s[2]),
            jnp.stack(new_s[3]), jnp.stack(new_s[4]), jnp.stack(new_s[5]), jnp.stack(new_s[6]))
```

```python
import functools

import jax
import jax.numpy as jnp
from jax import lax
from jax.experimental import pallas as pl
from jax.experimental.pallas import tpu as pltpu

D_MODEL = 1024
BATCH = 2
SEQ = 8192
DEPTH = 1
DEC_BATCH = 128
DEC_SEQ = 4
PAST_LEN = 2048

M_HEADS = 4
M_DQK = D_MODEL // 8
M_DV = D_MODEL // 4
CONV_W = 4
ATTN_GROUPS = ((128, 1), (512, 4), (2048, 16))
A_HEADS = 4
A_DH = D_MODEL // 16
N_EXPERTS = 32
TOP_K = 4
D_FF = D_MODEL
SWIGLU_LIMIT = 7.0
SWIGLU_ALPHA = 1.702
DN_ALPHA = (2 * DEPTH) ** 0.25
LN_EPS = 1e-5

QK_W = 2 * M_HEADS * M_DQK
MV_W = M_HEADS * M_DV
A_W = A_HEADS * A_DH
OFF_V = QK_W
OFF_I = OFF_V + MV_W
OFF_F = OFF_I + M_HEADS
OFF_O = OFF_F + M_HEADS
OFF_A = OFF_O + MV_W
N_GROUPS = len(ATTN_GROUPS)
P_W = OFF_A + 3 * A_W * N_GROUPS

LANES = 128
SUBLANES = 8
ATTN_BLK = 128
MLSTM_CHUNK = 64
TOK_TILE = 256
MOE_TILE = 256
VMEM_LIMIT = 56 * 1024 * 1024

SEG_QK = 0
SEG_V = SEG_QK + QK_W
SEG_O = SEG_V + MV_W
SEG_A = SEG_O + MV_W
SEG_G = SEG_A + 3 * A_W * N_GROUPS
SEG_IF = SEG_G + 2 * D_MODEL
W1_COLS = SEG_IF + LANES

F32 = jnp.float32
BF16 = jnp.bfloat16
NEG_INF = float("-inf")


def _ln(x):
    mu = jnp.mean(x, axis=-1, keepdims=True)
    xc = x - mu
    var = jnp.mean(xc * xc, axis=-1, keepdims=True)
    return xc * lax.rsqrt(var + LN_EPS)


def _sigmoid(x):
    return 1.0 / (1.0 + jnp.exp(-x))


def _nt_dot(a, b, precision=None):
    return lax.dot_general(a, b, (((1,), (1,)), ((), ())), preferred_element_type=F32, precision=precision)


def _tn_dot(a, b):
    return lax.dot_general(a, b, (((0,), (0,)), ((), ())), preferred_element_type=F32)


def _ada_kernel(c_ref, w_ref, b_ref, o_ref):
    c = c_ref[...]
    s = c * _sigmoid(c)
    o_ref[...] = jnp.dot(s.astype(BF16), w_ref[...].astype(BF16), preferred_element_type=F32) + b_ref[...]


def _ada(c_all, w_ada, b_ada):
    r = c_all.shape[0]
    n_col = 6 * D_MODEL // D_MODEL
    return pl.pallas_call(
        _ada_kernel,
        out_shape=jax.ShapeDtypeStruct((r, 6 * D_MODEL), F32),
        grid=(n_col,),
        in_specs=[pl.BlockSpec((r, D_MODEL), lambda j: (0, 0)),
                  pl.BlockSpec((D_MODEL, D_MODEL), lambda j: (0, j)),
                  pl.BlockSpec((1, D_MODEL), lambda j: (0, j))],
        out_specs=pl.BlockSpec((r, D_MODEL), lambda j: (0, j)),
        name="ada_mod",
    )(c_all, w_ada, b_ada.reshape(1, -1))


def _inproj_kernel(x_ref, sc_ref, sh_ref, w_ref, b_ref,
                   qk_ref, v_ref, o_ref, q0_ref, kv0_ref, q1_ref, kv1_ref, q2_ref, kv2_ref, g_ref, if_ref):
    h = _ln(x_ref[...]) * (1.0 + sc_ref[...]) + sh_ref[...]
    hb = h.astype(BF16)

    def seg(a, width):
        return jnp.dot(hb, w_ref[:, a:a + width], preferred_element_type=F32) + b_ref[:, a:a + width]

    qk_ref[...] = seg(SEG_QK, QK_W)
    v_ref[...] = seg(SEG_V, MV_W)
    o_ref[...] = seg(SEG_O, MV_W)
    for gi, (q_ref, kv_ref) in enumerate(((q0_ref, kv0_ref), (q1_ref, kv1_ref), (q2_ref, kv2_ref))):
        base = SEG_A + 3 * A_W * gi
        q_ref[...] = seg(base, A_W)
        kv_ref[...] = seg(base + A_W, 2 * A_W)
    g_ref[...] = _sigmoid(seg(SEG_G, 2 * D_MODEL))
    if_ref[...] = seg(SEG_IF, LANES)


def _mod_spec(per_token, tiles_per_batch):
    if per_token:
        return pl.BlockSpec((TOK_TILE, D_MODEL), lambda i: (i, 0))
    return pl.BlockSpec((None, 1, D_MODEL), lambda i: (i // tiles_per_batch, 0, 0))


def _inproj(x2d, sc, sh, w1, b1, per_token, tiles_per_batch):
    n = x2d.shape[0]
    tok = lambda width: pl.BlockSpec((TOK_TILE, width), lambda i: (i, 0))
    widths = (QK_W, MV_W, MV_W, A_W, 2 * A_W, A_W, 2 * A_W, A_W, 2 * A_W, 2 * D_MODEL, LANES)
    return pl.pallas_call(
        _inproj_kernel,
        out_shape=[jax.ShapeDtypeStruct((n, w), F32) for w in widths],
        grid=(n // TOK_TILE,),
        in_specs=[tok(D_MODEL), _mod_spec(per_token, tiles_per_batch), _mod_spec(per_token, tiles_per_batch),
                  pl.BlockSpec((D_MODEL, W1_COLS), lambda i: (0, 0), pipeline_mode=pl.Buffered(1)),
                  pl.BlockSpec((1, W1_COLS), lambda i: (0, 0))],
        out_specs=[tok(w) for w in widths],
        compiler_params=pltpu.CompilerParams(vmem_limit_bytes=VMEM_LIMIT),
        name="inproj",
    )(x2d, sc, sh, w1, b1)


def _mlstm_kernel(qk_ref, v_ref, o_ref, if_ref, cw_ref, cb_ref, nw_ref, hist0_ref, c0_ref, n0_ref, m0_ref,
                  hm_ref, cout_ref, nout_ref, mout_ref, hout_ref,
                  xbuf, vbuf, obuf, gbuf, c_s, n_s, m_s, *, chunk, valid):
    c_idx = pl.program_id(1)
    last = pl.num_programs(1) - 1

    @pl.when(c_idx == 0)
    def _():
        xbuf[0:SUBLANES, :] = hist0_ref[...]
        c_s[...] = c0_ref[...]
        n_s[...] = n0_ref[...]
        m_s[...] = m0_ref[...]

    if valid < chunk:
        xbuf[SUBLANES:SUBLANES + chunk, :] = jnp.zeros((chunk, QK_W), F32)
        vbuf[...] = jnp.zeros((chunk, MV_W), F32)
        obuf[...] = jnp.zeros((chunk, MV_W), F32)
        gbuf[...] = jnp.zeros((chunk, LANES), F32)
        xbuf[SUBLANES:SUBLANES + valid, :] = qk_ref[...]
        vbuf[0:valid, :] = v_ref[...]
        obuf[0:valid, :] = o_ref[...]
        gbuf[0:valid, :] = if_ref[...]
        v_all, o_all, g_raw = vbuf[...], obuf[...], gbuf[...]
    else:
        xbuf[SUBLANES:SUBLANES + chunk, :] = qk_ref[...]
        v_all, o_all, g_raw = v_ref[...], o_ref[...], if_ref[...]

    conv = cb_ref[...]
    for j in range(CONV_W):
        off = SUBLANES - (CONV_W - 1) + j
        conv = conv + cw_ref[j:j + 1, :] * xbuf[off:off + chunk, :]
    new_hist = xbuf[valid:valid + SUBLANES, :]
    xbuf[0:SUBLANES, :] = new_hist
    act = conv * _sigmoid(conv)
    if valid < chunk:
        rows = lax.broadcasted_iota(jnp.int32, (chunk, 1), 0)
        act = jnp.where(rows < valid, act, 0.0)

    lane = lax.broadcasted_iota(jnp.int32, (chunk, LANES), 1)
    is_f = (lane >= M_HEADS) & (lane < 2 * M_HEADS)
    lf = jnp.minimum(g_raw, 0.0) - jnp.log(1.0 + jnp.exp(-jnp.abs(g_raw)))
    g_lin = jnp.where(is_f, lf, g_raw)
    if valid < chunk:
        rows_l = lax.broadcasted_iota(jnp.int32, (chunk, LANES), 0)
        g_lin = jnp.where((rows_l >= valid) & is_f, 0.0, g_lin)
    ri = lax.broadcasted_iota(jnp.int32, (chunk, chunk), 0)
    ci = lax.broadcasted_iota(jnp.int32, (chunk, chunk), 1)
    causal = ci <= ri
    tri = causal.astype(F32)
    csum = jnp.dot(tri, g_lin, preferred_element_type=F32, precision=lax.Precision.HIGHEST)
    m_col = jnp.where(is_f, csum, g_lin)
    sel = (lax.broadcasted_iota(jnp.int32, (SUBLANES, LANES), 0)
           == lax.broadcasted_iota(jnp.int32, (SUBLANES, LANES), 1)).astype(F32)
    m_row = _nt_dot(sel, m_col, precision=lax.Precision.HIGHEST)
    if valid < chunk:
        tcol = lax.broadcasted_iota(jnp.int32, (chunk, 1), 0)
        trow = lax.broadcasted_iota(jnp.int32, (1, chunk), 1)

    for h in range(M_HEADS):
        q = act[:, h * M_DQK:(h + 1) * M_DQK]
        k = act[:, QK_W // 2 + h * M_DQK:QK_W // 2 + (h + 1) * M_DQK] * (M_DQK ** -0.5)
        v = v_all[:, h * M_DV:(h + 1) * M_DV]
        qb, kb, vb = q.astype(BF16), k.astype(BF16), v.astype(BF16)
        li_col = m_col[:, h:h + 1]
        b_col = m_col[:, M_HEADS + h:M_HEADS + h + 1]
        li_row = m_row[h:h + 1, :]
        b_row = m_row[M_HEADS + h:M_HEADS + h + 1, :]
        if valid < chunk:
            li_col = jnp.where(tcol < valid, li_col, NEG_INF)
            li_row = jnp.where(trow < valid, li_row, NEG_INF)
        m_prev = m_s[h:h + 1, 0:1]
        c_prev = c_s[h]
        n_prev = n_s[h:h + 1, :]

        dmat = jnp.where(causal, b_col - b_row + li_row, NEG_INF)
        inter = b_col + m_prev
        m_t = jnp.maximum(inter, jnp.max(dmat, axis=-1, keepdims=True))
        s_qk = _nt_dot(qb, kb)
        a = jnp.exp(dmat - m_t) * s_qk
        sc = jnp.exp(inter - m_t)
        num = sc * jnp.dot(qb, c_prev.astype(BF16), preferred_element_type=F32) \
            + jnp.dot(a.astype(BF16), vb, preferred_element_type=F32)
        den = sc * jnp.sum(q * n_prev, axis=-1, keepdims=True) + jnp.sum(a, axis=-1, keepdims=True)
        hh = num / jnp.maximum(jnp.abs(den), jnp.exp(-m_t))

        b_end = b_col[chunk - 1:chunk, :]
        g_col = b_end - b_col + li_col
        g_row = b_end - b_row + li_row
        m_new = jnp.maximum(b_end + m_prev, jnp.max(g_row, axis=-1, keepdims=True))
        w_col = jnp.exp(g_col - m_new)
        decay = jnp.exp(b_end + m_prev - m_new)
        c_s[h] = decay * c_prev + _tn_dot(kb, (w_col * v).astype(BF16))
        n_s[h:h + 1, :] = decay * n_prev + jnp.sum(w_col * k, axis=0, keepdims=True)
        m_s[h:h + 1, :] = jnp.broadcast_to(m_new, (1, LANES))

        y = _ln(hh) * nw_ref[:, h * M_DV:(h + 1) * M_DV] * _sigmoid(o_all[:, h * M_DV:(h + 1) * M_DV])
        hm_ref[:, h * M_DV:(h + 1) * M_DV] = y[0:valid, :]

    @pl.when(c_idx == last)
    def _():
        cout_ref[...] = c_s[...]
        nout_ref[...] = n_s[...]
        mout_ref[...] = m_s[...]
        hout_ref[...] = xbuf[0:SUBLANES, :]


def _mlstm(qk, v, o, ifg, conv_w, conv_b, norm_w, hist0, c0, n0, m0, *, n_seq, n_chunks, chunk, valid):
    if valid == chunk:
        tok = lambda width: pl.BlockSpec((chunk, width), lambda b, c: (b * n_chunks + c, 0))
        args = (qk, v, o, ifg)
        hm_shape = jax.ShapeDtypeStruct((n_seq * n_chunks * valid, MV_W), F32)
    else:
        tok = lambda width: pl.BlockSpec((None, valid, width), lambda b, c: (b * n_chunks + c, 0, 0))
        args = tuple(a.reshape(n_seq * n_chunks, valid, a.shape[-1]) for a in (qk, v, o, ifg))
        hm_shape = jax.ShapeDtypeStruct((n_seq * n_chunks, valid, MV_W), F32)
    const2 = lambda r, w: pl.BlockSpec((r, w), lambda b, c: (0, 0))
    per_seq = lambda *dims: pl.BlockSpec((None,) + dims, lambda b, c: (b,) + (0,) * len(dims))
    outs = pl.pallas_call(
        functools.partial(_mlstm_kernel, chunk=chunk, valid=valid),
        out_shape=[hm_shape,
                   jax.ShapeDtypeStruct((n_seq, M_HEADS, M_DQK, M_DV), F32),
                   jax.ShapeDtypeStruct((n_seq, SUBLANES, M_DQK), F32),
                   jax.ShapeDtypeStruct((n_seq, SUBLANES, LANES), F32),
                   jax.ShapeDtypeStruct((n_seq, SUBLANES, QK_W), F32)],
        grid=(n_seq, n_chunks),
        in_specs=[tok(QK_W), tok(MV_W), tok(MV_W), tok(LANES),
                  const2(CONV_W, QK_W), const2(1, QK_W), const2(1, MV_W),
                  per_seq(SUBLANES, QK_W), per_seq(M_HEADS, M_DQK, M_DV), per_seq(SUBLANES, M_DQK),
                  per_seq(SUBLANES, LANES)],
        out_specs=[tok(MV_W), per_seq(M_HEADS, M_DQK, M_DV), per_seq(SUBLANES, M_DQK),
                   per_seq(SUBLANES, LANES), per_seq(SUBLANES, QK_W)],
        scratch_shapes=[pltpu.VMEM((chunk + 2 * SUBLANES, QK_W), F32), pltpu.VMEM((chunk, MV_W), F32),
                        pltpu.VMEM((chunk, MV_W), F32), pltpu.VMEM((chunk, LANES), F32),
                        pltpu.VMEM((M_HEADS, M_DQK, M_DV), F32), pltpu.VMEM((SUBLANES, M_DQK), F32),
                        pltpu.VMEM((SUBLANES, LANES), F32)],
        compiler_params=pltpu.CompilerParams(dimension_semantics=("arbitrary", "arbitrary"),
                                             vmem_limit_bytes=VMEM_LIMIT),
        name="mlstm",
    )(*args, conv_w, conv_b.reshape(1, -1), norm_w.reshape(1, -1), hist0, c0, n0, m0)
    hm = outs[0].reshape(n_seq * n_chunks * valid, MV_W)
    return hm, outs[1], outs[2], outs[3], outs[4]


def _pad_rows(a, rows, at_end):
    pad = rows - a.shape[1]
    cfg = ((0, 0), (pad, 0), (0, 0)) if at_end else ((0, 0), (0, pad), (0, 0))
    return jnp.pad(a, cfg)


def _attn_prompt_kernel(q_ref, kvc_ref, kvp_ref, o_ref, l_ref):
    n = pl.program_id(2)
    qi = lax.broadcasted_iota(jnp.int32, (ATTN_BLK, ATTN_BLK), 0)
    ki = lax.broadcasted_iota(jnp.int32, (ATTN_BLK, ATTN_BLK), 1)
    mask_c = ki <= qi
    mask_p = (ki >= qi) & (n > 0)
    q = q_ref[...] * (A_DH ** -0.5)
    kvc = kvc_ref[...]
    kvp = kvp_ref[...]
    outs, lses = [], []
    for h in range(A_HEADS):
        sl = slice(h * A_DH, (h + 1) * A_DH)
        slv = slice(A_W + h * A_DH, A_W + (h + 1) * A_DH)
        qh = q[:, sl].astype(BF16)
        s_c = jnp.where(mask_c, _nt_dot(qh, kvc[:, sl].astype(BF16)), NEG_INF)
        s_p = jnp.where(mask_p, _nt_dot(qh, kvp[:, sl].astype(BF16)), NEG_INF)
        mx = jnp.maximum(jnp.max(s_c, axis=-1, keepdims=True), jnp.max(s_p, axis=-1, keepdims=True))
        p_c = jnp.exp(s_c - mx)
        p_p = jnp.exp(s_p - mx)
        l = jnp.sum(p_c, axis=-1, keepdims=True) + jnp.sum(p_p, axis=-1, keepdims=True)
        inv = 1.0 / l
        pv = jnp.dot((p_c * inv).astype(BF16), kvc[:, slv].astype(BF16), preferred_element_type=F32) \
            + jnp.dot((p_p * inv).astype(BF16), kvp[:, slv].astype(BF16), preferred_element_type=F32)
        outs.append(pv)
        lses.append(jnp.broadcast_to(mx + jnp.log(l), (ATTN_BLK, A_DH)))
    o_ref[...] = jnp.concatenate(outs, axis=-1)
    l_ref[...] = jnp.concatenate(lses, axis=-1)


def _attn_prompt(q, kv, dilation):
    t_d = SEQ // dilation
    nb = t_d // ATTN_BLK
    q3 = q.reshape(BATCH, t_d, dilation * A_W)
    kv3 = kv.reshape(BATCH, t_d, dilation * 2 * A_W)
    qspec = pl.BlockSpec((None, ATTN_BLK, A_W), lambda b, r, n: (b, n, r))
    o, l = pl.pallas_call(
        _attn_prompt_kernel,
        out_shape=[jax.ShapeDtypeStruct(q3.shape, F32)] * 2,
        grid=(BATCH, dilation, nb),
        in_specs=[qspec,
                  pl.BlockSpec((None, ATTN_BLK, 2 * A_W), lambda b, r, n: (b, n, r)),
                  pl.BlockSpec((None, ATTN_BLK, 2 * A_W), lambda b, r, n: (b, jnp.maximum(n - 1, 0), r))],
        out_specs=[qspec, qspec],
        name="attn_prompt",
    )(q3, kv3, kv3)
    return o.reshape(BATCH * SEQ, A_W), l.reshape(BATCH * SEQ, A_W)


N_QROWS = A_HEADS * DEC_SEQ


def _attn_sample_group(q, kv_new, cache_ref, window, dilation, qexp, obuf):
    n_slab = min(dilation, DEC_SEQ)
    cache_len = min(window, PAST_LEN)
    rows = cache_ref.shape[0]
    lane_head = lax.broadcasted_iota(jnp.int32, (DEC_SEQ, A_W), 1) // A_DH
    qs = q * (A_DH ** -0.5)
    for h in range(A_HEADS):
        qexp[h * DEC_SEQ:(h + 1) * DEC_SEQ, :] = jnp.where(lane_head == h, qs, 0.0)
    qe = qexp[...].astype(BF16)
    tok = lax.broadcasted_iota(jnp.int32, (N_QROWS, 1), 0) % DEC_SEQ

    s_parts, v_parts = [], []
    for r in range(n_slab):
        kc = cache_ref[:, r * 2 * A_W:r * 2 * A_W + A_W]
        vc = cache_ref[:, r * 2 * A_W + A_W:(r + 1) * 2 * A_W]
        pos = lax.broadcasted_iota(jnp.int32, (1, rows), 1) * dilation + r
        delta = cache_len + tok - pos
        ok = ((delta & (dilation - 1)) == 0) & (delta <= window) & (delta >= dilation)
        s_parts.append(jnp.where(ok, _nt_dot(qe, kc.astype(BF16)), NEG_INF))
        v_parts.append(vc.astype(BF16))
    kn = kv_new[:, 0:A_W]
    vn = kv_new[:, A_W:2 * A_W]
    obuf[...] = jnp.zeros((N_QROWS, A_W), F32)
    obuf[0:DEC_SEQ, :] = kn
    obuf[SUBLANES:SUBLANES + DEC_SEQ, :] = vn
    knp = obuf[0:SUBLANES, :].astype(BF16)
    vnp = obuf[SUBLANES:2 * SUBLANES, :].astype(BF16)
    npos = lax.broadcasted_iota(jnp.int32, (1, SUBLANES), 1)
    dnew = tok - npos
    ok_n = (dnew >= 0) & ((dnew & (dilation - 1)) == 0) & (dnew <= window) & (npos < DEC_SEQ)
    s_n = jnp.where(ok_n, _nt_dot(qe, knp), NEG_INF)

    mx = jnp.max(s_n, axis=-1, keepdims=True)
    for s in s_parts:
        mx = jnp.maximum(mx, jnp.max(s, axis=-1, keepdims=True))
    p_n = jnp.exp(s_n - mx)
    l = jnp.sum(p_n, axis=-1, keepdims=True)
    p_parts = []
    for s in s_parts:
        p = jnp.exp(s - mx)
        l = l + jnp.sum(p, axis=-1, keepdims=True)
        p_parts.append(p)
    inv = 1.0 / l
    acc = jnp.dot((p_n * inv).astype(BF16), vnp, preferred_element_type=F32)
    for p, vcb in zip(p_parts, v_parts):
        acc = acc + jnp.dot((p * inv).astype(BF16), vcb, preferred_element_type=F32)
    lse = mx + jnp.log(l)
    row_head = lax.broadcasted_iota(jnp.int32, (N_QROWS, A_W), 0) // DEC_SEQ
    lane_head16 = lax.broadcasted_iota(jnp.int32, (N_QROWS, A_W), 1) // A_DH
    keep = row_head == lane_head16
    obuf[...] = jnp.where(keep, acc, 0.0)
    o = obuf[0:DEC_SEQ, :]
    for h in range(1, A_HEADS):
        o = o + obuf[h * DEC_SEQ:(h + 1) * DEC_SEQ, :]
    obuf[...] = jnp.where(keep, lse, 0.0)
    le = obuf[0:DEC_SEQ, :]
    for h in range(1, A_HEADS):
        le = le + obuf[h * DEC_SEQ:(h + 1) * DEC_SEQ, :]
    return o, le


def _attn_sample_kernel(q0_ref, kv0_ref, c0_ref, q1_ref, kv1_ref, c1_ref, q2_ref, kv2_ref, c2_ref,
                        o0_ref, l0_ref, o1_ref, l1_ref, o2_ref, l2_ref, qexp, obuf):
    ins = ((q0_ref, kv0_ref, c0_ref, o0_ref, l0_ref), (q1_ref, kv1_ref, c1_ref, o1_ref, l1_ref),
           (q2_ref, kv2_ref, c2_ref, o2_ref, l2_ref))
    for (window, dilation), (q_ref, kv_ref, c_ref, o_ref, l_ref) in zip(ATTN_GROUPS, ins):
        o, le = _attn_sample_group(q_ref[...], kv_ref[...], c_ref, window, dilation, qexp, obuf)
        o_ref[...] = o
        l_ref[...] = le


def _attn_sample(qs, kvs, caches):
    args, in_specs = [], []
    tok = lambda width: pl.BlockSpec((None, DEC_SEQ, width), lambda b: (b, 0, 0))
    for (window, dilation), q, kv, cache in zip(ATTN_GROUPS, qs, kvs, caches):
        cache_len = min(window, PAST_LEN)
        n_slab = min(dilation, DEC_SEQ)
        c3 = cache.reshape(DEC_BATCH, cache_len // dilation, dilation * 2 * A_W)
        args += [q.reshape(DEC_BATCH, DEC_SEQ, A_W), kv.reshape(DEC_BATCH, DEC_SEQ, 2 * A_W), c3]
        in_specs += [tok(A_W), tok(2 * A_W),
                     pl.BlockSpec((None, cache_len // dilation, n_slab * 2 * A_W), lambda b: (b, 0, 0))]
    outs = pl.pallas_call(
        _attn_sample_kernel,
        out_shape=[jax.ShapeDtypeStruct((DEC_BATCH, DEC_SEQ, A_W), F32)] * (2 * N_GROUPS),
        grid=(DEC_BATCH,),
        in_specs=in_specs,
        out_specs=[tok(A_W)] * (2 * N_GROUPS),
        scratch_shapes=[pltpu.VMEM((N_QROWS, A_W), F32), pltpu.VMEM((N_QROWS, A_W), F32)],
        name="attn_sample",
    )(*args)
    return [a.reshape(DEC_BATCH * DEC_SEQ, A_W) for a in outs]


def _merge_kernel(hm_ref, o0_ref, l0_ref, o1_ref, l1_ref, o2_ref, l2_ref, g_ref, x_ref, g1_ref, sc2_ref, sh2_ref,
                  wbm_ref, wba_ref, wo_ref, lng_ref, lnb_ref, wr_ref, br_ref, cnt0_ref,
                  x1_ref, h2_ref, ei_ref, eg_ref, er_ref, cnt_ref, run_s):
    i = pl.program_id(0)

    @pl.when(i == 0)
    def _():
        run_s[...] = cnt0_ref[...]

    l0, l1, l2 = l0_ref[...], l1_ref[...], l2_ref[...]
    mx = jnp.maximum(jnp.maximum(l0, l1), l2)
    e0, e1, e2 = jnp.exp(l0 - mx), jnp.exp(l1 - mx), jnp.exp(l2 - mx)
    inv = 1.0 / (e0 + e1 + e2)
    attn = (e0 * inv) * o0_ref[...] + (e1 * inv) * o1_ref[...] + (e2 * inv) * o2_ref[...]
    br_m = jnp.dot(hm_ref[...].astype(BF16), wbm_ref[...], preferred_element_type=F32)
    br_a = jnp.dot(attn.astype(BF16), wba_ref[...], preferred_element_type=F32)
    merged = g_ref[:, 0:D_MODEL] * br_m + g_ref[:, D_MODEL:2 * D_MODEL] * br_a
    mix = jnp.dot(merged.astype(BF16), wo_ref[...], preferred_element_type=F32)
    x1 = _ln(DN_ALPHA * x_ref[...] + g1_ref[...] * mix) * lng_ref[...] + lnb_ref[...]
    x1_ref[...] = x1
    h2 = _ln(x1) * (1.0 + sc2_ref[...]) + sh2_ref[...]
    h2_ref[...] = h2

    lane = lax.broadcasted_iota(jnp.int32, (TOK_TILE, LANES), 1)
    lane_f = lane.astype(F32)
    logits = jnp.dot(h2.astype(BF16), wr_ref[...], preferred_element_type=F32) + br_ref[...]
    cur = jnp.where(lane < N_EXPERTS, logits, NEG_INF)
    vals, onehots = [], []
    idx_mat = jnp.zeros((TOK_TILE, LANES), F32)
    for k in range(TOP_K):
        m = jnp.max(cur, axis=-1, keepdims=True)
        sel = jnp.min(jnp.where(cur == m, lane_f, float(LANES)), axis=-1, keepdims=True)
        hit = lane_f == sel
        vals.append(m)
        onehots.append(hit)
        idx_mat = jnp.where(lane == k, sel, idx_mat)
        cur = jnp.where(hit, NEG_INF, cur)
    es = [jnp.exp(v - vals[0]) for v in vals]
    tot = es[0] + es[1] + es[2] + es[3]
    gate_mat = jnp.zeros((TOK_TILE, LANES), F32)
    for k in range(TOP_K):
        gate_mat = jnp.where(lane == k, es[k] / tot, gate_mat)

    cnt = jnp.zeros((TOK_TILE, LANES), F32)
    for hit in onehots:
        cnt = cnt + hit.astype(F32)
    ri = lax.broadcasted_iota(jnp.int32, (TOK_TILE, TOK_TILE), 0)
    ci = lax.broadcasted_iota(jnp.int32, (TOK_TILE, TOK_TILE), 1)
    before = (ci < ri).astype(BF16)
    base = run_s[0:1, :] + jnp.dot(before, cnt.astype(BF16), preferred_element_type=F32)
    rank_mat = jnp.zeros((TOK_TILE, LANES), F32)
    for k, hit in enumerate(onehots):
        rk = jnp.sum(jnp.where(hit, base, 0.0), axis=-1, keepdims=True)
        rank_mat = jnp.where(lane == k, rk, rank_mat)
    new_run = run_s[0:1, :] + jnp.sum(cnt, axis=0, keepdims=True)
    run_s[...] = jnp.broadcast_to(new_run, (SUBLANES, LANES))
    cnt_ref[...] = run_s[...]
    ei_ref[...] = idx_mat[:, 0:TOP_K].astype(jnp.int32)
    eg_ref[...] = gate_mat[:, 0:TOP_K]
    er_ref[...] = rank_mat[:, 0:TOP_K].astype(jnp.int32)


def _merge(hm, attn_o, attn_l, gates, x2d, g1, sc2, sh2, wts, cnt0, per_token, tiles_per_batch):
    n = x2d.shape[0]
    tok = lambda width: pl.BlockSpec((TOK_TILE, width), lambda i: (i, 0))
    const = lambda a: pl.BlockSpec(a.shape, lambda i: (0,) * a.ndim)
    mod = _mod_spec(per_token, tiles_per_batch)
    attn_args, attn_specs = [], []
    for o, l in zip(attn_o, attn_l):
        attn_args += [o, l]
        attn_specs += [tok(A_W), tok(A_W)]
    return pl.pallas_call(
        _merge_kernel,
        out_shape=[jax.ShapeDtypeStruct((n, D_MODEL), F32), jax.ShapeDtypeStruct((n, D_MODEL), F32),
                   jax.ShapeDtypeStruct((n, TOP_K), jnp.int32), jax.ShapeDtypeStruct((n, TOP_K), F32),
                   jax.ShapeDtypeStruct((n, TOP_K), jnp.int32), jax.ShapeDtypeStruct((SUBLANES, LANES), F32)],
        grid=(n // TOK_TILE,),
        in_specs=[tok(MV_W)] + attn_specs + [tok(2 * D_MODEL), tok(D_MODEL), mod, mod, mod]
                 + [const(w) for w in wts] + [const(cnt0)],
        out_specs=[tok(D_MODEL), tok(D_MODEL), tok(TOP_K), tok(TOP_K), tok(TOP_K),
                   pl.BlockSpec((SUBLANES, LANES), lambda i: (0, 0))],
        scratch_shapes=[pltpu.VMEM((SUBLANES, LANES), F32)],
        compiler_params=pltpu.CompilerParams(dimension_semantics=("arbitrary",), vmem_limit_bytes=VMEM_LIMIT),
        name="merge_router",
    )(hm, *attn_args, gates, x2d, g1, sc2, sh2, *wts, cnt0)


def _row_copy(src_hbm, src_row, dst, dst_row, sem):
    return pltpu.make_async_copy(src_hbm.at[pl.ds(src_row, 1)], dst.at[pl.ds(dst_row, 1)], sem)


def _dispatch_kernel(dest_ref, h2_hbm, xs_in_hbm, xs_hbm, sem):
    del xs_in_hbm
    base = pl.program_id(0) * TOK_TILE

    def issue(j, carry):
        _row_copy(h2_hbm, base + j // TOP_K, xs_hbm, dest_ref[j], sem).start()
        return carry

    def drain(j, carry):
        _row_copy(h2_hbm, 0, xs_hbm, 0, sem).wait()
        return carry

    lax.fori_loop(0, TOK_TILE * TOP_K, issue, 0)
    lax.fori_loop(0, TOK_TILE * TOP_K, drain, 0)


def _dispatch(dest_flat, h2, n_rows):
    n = h2.shape[0]
    xs0 = jnp.zeros((n_rows, D_MODEL), F32)
    return pl.pallas_call(
        _dispatch_kernel,
        out_shape=jax.ShapeDtypeStruct((n_rows, D_MODEL), F32),
        grid=(n // TOK_TILE,),
        in_specs=[pl.BlockSpec((TOK_TILE * TOP_K,), lambda i: (i,), memory_space=pltpu.SMEM),
                  pl.BlockSpec(memory_space=pl.ANY), pl.BlockSpec(memory_space=pl.ANY)],
        out_specs=pl.BlockSpec(memory_space=pl.ANY),
        scratch_shapes=[pltpu.SemaphoreType.DMA(())],
        input_output_aliases={2: 0},
        compiler_params=pltpu.CompilerParams(dimension_semantics=("arbitrary",)),
        name="moe_dispatch",
    )(dest_flat, h2, xs0)


def _expert_kernel(be_ref, bv_ref, xs_ref, wu_ref, bu_ref, wd_ref, bd_ref, ys_ref):
    i = pl.program_id(0)

    @pl.when(bv_ref[i] == 1)
    def _():
        hu = jnp.dot(xs_ref[...].astype(BF16), wu_ref[...], preferred_element_type=F32) + bu_ref[...]
        x_glu = jnp.minimum(hu[:, 0:D_FF], SWIGLU_LIMIT)
        x_lin = jnp.clip(hu[:, D_FF:2 * D_FF], -SWIGLU_LIMIT, SWIGLU_LIMIT)
        act = x_glu * _sigmoid(SWIGLU_ALPHA * x_glu) * (x_lin + 1.0)
        ys_ref[...] = jnp.dot(act.astype(BF16), wd_ref[...], preferred_element_type=F32) + bd_ref[...]

    @pl.when(bv_ref[i] == 0)
    def _():
        ys_ref[...] = jnp.zeros((MOE_TILE, D_MODEL), F32)


def _experts(blk_e, blk_valid, xs, w_up, b_up, w_down, b_down):
    n_blk = blk_e.shape[0]
    row = lambda i, be, bv: (i, 0)
    return pl.pallas_call(
        _expert_kernel,
        out_shape=jax.ShapeDtypeStruct(xs.shape, F32),
        grid_spec=pltpu.PrefetchScalarGridSpec(
            num_scalar_prefetch=2,
            grid=(n_blk,),
            in_specs=[pl.BlockSpec((MOE_TILE, D_MODEL), row),
                      pl.BlockSpec((None, D_MODEL, 2 * D_FF), lambda i, be, bv: (be[i], 0, 0)),
                      pl.BlockSpec((None, 1, 2 * D_FF), lambda i, be, bv: (be[i], 0, 0)),
                      pl.BlockSpec((None, D_FF, D_MODEL), lambda i, be, bv: (be[i], 0, 0)),
                      pl.BlockSpec((None, 1, D_MODEL), lambda i, be, bv: (be[i], 0, 0))],
            out_specs=pl.BlockSpec((MOE_TILE, D_MODEL), row)),
        compiler_params=pltpu.CompilerParams(dimension_semantics=("arbitrary",), vmem_limit_bytes=VMEM_LIMIT),
        name="moe_experts",
    )(blk_e, blk_valid, xs, w_up, b_up.reshape(N_EXPERTS, 1, -1), w_down, b_down.reshape(N_EXPERTS, 1, -1))


def _combine_kernel(dest_ref, ys_hbm, gate_ref, x1_ref, g2_ref, lng_ref, lnb_ref, out_ref, rows, sem):
    def issue(j, carry):
        _row_copy(ys_hbm, dest_ref[j], rows.at[j % TOP_K], j // TOP_K, sem).start()
        return carry

    def drain(j, carry):
        _row_copy(ys_hbm, 0, rows.at[0], 0, sem).wait()
        return carry

    lax.fori_loop(0, TOK_TILE * TOP_K, issue, 0)
    lax.fori_loop(0, TOK_TILE * TOP_K, drain, 0)
    gate = gate_ref[...]
    f = gate[:, 0:1] * rows[0]
    for k in range(1, TOP_K):
        f = f + gate[:, k:k + 1] * rows[k]
    out_ref[...] = _ln(DN_ALPHA * x1_ref[...] + g2_ref[...] * f) * lng_ref[...] + lnb_ref[...]


def _combine(dest_flat, ys, gate, x1, g2, ln_g, ln_b, per_token, tiles_per_batch):
    n = x1.shape[0]
    tok = lambda width: pl.BlockSpec((TOK_TILE, width), lambda i: (i, 0))
    const = pl.BlockSpec((1, D_MODEL), lambda i: (0, 0))
    return pl.pallas_call(
        _combine_kernel,
        out_shape=jax.ShapeDtypeStruct((n, D_MODEL), F32),
        grid=(n // TOK_TILE,),
        in_specs=[pl.BlockSpec((TOK_TILE * TOP_K,), lambda i: (i,), memory_space=pltpu.SMEM),
                  pl.BlockSpec(memory_space=pl.ANY), tok(TOP_K), tok(D_MODEL),
                  _mod_spec(per_token, tiles_per_batch), const, const],
        out_specs=tok(D_MODEL),
        scratch_shapes=[pltpu.VMEM((TOP_K, TOK_TILE, D_MODEL), F32), pltpu.SemaphoreType.DMA(())],
        compiler_params=pltpu.CompilerParams(dimension_semantics=("arbitrary",), vmem_limit_bytes=VMEM_LIMIT),
        name="moe_combine",
    )(dest_flat, ys, gate, x1, g2, ln_g.reshape(1, -1), ln_b.reshape(1, -1))


def _routing(e_idx, rank, counts):
    n_assign = e_idx.shape[0] * TOP_K
    n_blk = n_assign // MOE_TILE + N_EXPERTS
    padded = (counts + MOE_TILE - 1) // MOE_TILE * MOE_TILE
    pend = jnp.cumsum(padded)
    pstart = pend - padded
    dest = pstart[e_idx] + rank
    blk_row = jnp.arange(n_blk, dtype=jnp.int32) * MOE_TILE
    blk_e = jnp.clip(jnp.searchsorted(pend, blk_row, side="right"), 0, N_EXPERTS - 1).astype(jnp.int32)
    blk_valid = (blk_row < pend[-1]).astype(jnp.int32)
    return dest.reshape(-1).astype(jnp.int32), blk_e, blk_valid, n_blk * MOE_TILE


def kernel(x_prompt, x_sample, c_prompt, c_sample, cache_kv_w128, cache_kv_w512, cache_kv_w2048,
           state_mlstm_C, state_mlstm_n, state_mlstm_m, state_qk_conv,
           w_ada, b_ada, w_in, b_in, conv_w, conv_b, mh_norm_w, w_br_mlstm, w_br_attn,
           w_merge, b_merge, w_o, ln1_g, ln1_b, w_router, b_router, w_up, b_up, w_down, b_down,
           ln2_g, ln2_b):
    assert DEPTH == 1 and SEQ % (ATTN_GROUPS[-1][0]) == 0 and SEQ % MLSTM_CHUNK == 0
    n_p, n_s = BATCH * SEQ, DEC_BATCH * DEC_SEQ
    tiles_per_batch = SEQ // TOK_TILE
    layer = 0

    wi, bi = w_in[layer], b_in[layer]
    pad_if = LANES - 2 * M_HEADS
    w1 = jnp.concatenate([wi[:, :OFF_V], wi[:, OFF_V:OFF_I], wi[:, OFF_O:OFF_A], wi[:, OFF_A:], w_merge[layer],
                          wi[:, OFF_I:OFF_O], jnp.zeros((D_MODEL, pad_if), F32)], axis=1).astype(BF16)
    b1 = jnp.concatenate([bi[:OFF_V], bi[OFF_V:OFF_I], bi[OFF_O:OFF_A], bi[OFF_A:], b_merge[layer],
                          bi[OFF_I:OFF_O], jnp.zeros((pad_if,), F32)]).reshape(1, -1)
    wr = jnp.pad(w_router[layer], ((0, 0), (0, LANES - N_EXPERTS))).astype(BF16)
    br = jnp.pad(b_router[layer], (0, LANES - N_EXPERTS)).reshape(1, -1)
    merge_w = (w_br_mlstm[layer].astype(BF16), w_br_attn[layer].astype(BF16), w_o[layer].astype(BF16),
               ln1_g[layer].reshape(1, -1), ln1_b[layer].reshape(1, -1), wr, br)

    mod = _ada(jnp.concatenate([c_prompt, c_sample], axis=0), w_ada[layer], b_ada[layer])
    mod_p = mod[:BATCH].reshape(BATCH, 6, 1, D_MODEL)
    mod_s = jnp.repeat(mod[BATCH:], DEC_SEQ, axis=0).reshape(n_s, 6, D_MODEL)
    mods_p = [mod_p[:, j] for j in range(6)]
    mods_s = [mod_s[:, j] for j in range(6)]

    results = {}
    cnt = jnp.zeros((SUBLANES, LANES), F32)
    stage = []
    for name, x, mods, per_token in (("p", x_prompt, mods_p, False), ("s", x_sample, mods_s, True)):
        x2d = x.reshape(-1, D_MODEL)
        qk, v, o, q0, kv0, q1, kv1, q2, kv2, gates, ifg = _inproj(x2d, mods[1], mods[0], w1, b1, per_token,
                                                                    tiles_per_batch)
        qs, kvs = (q0, q1, q2), (kv0, kv1, kv2)
        if name == "p":
            zeros = lambda *s: jnp.zeros(s, F32)
            hm, c_new, n_new, m_new, hist = _mlstm(
                qk, v, o, ifg, conv_w[layer], conv_b[layer], mh_norm_w[layer],
                zeros(BATCH, SUBLANES, QK_W), zeros(BATCH, M_HEADS, M_DQK, M_DV), zeros(BATCH, SUBLANES, M_DQK),
                zeros(BATCH, SUBLANES, LANES), n_seq=BATCH, n_chunks=SEQ // MLSTM_CHUNK, chunk=MLSTM_CHUNK,
                valid=MLSTM_CHUNK)
            attn = [_attn_prompt(q, kv, dil) for q, kv, (_, dil) in zip(qs, kvs, ATTN_GROUPS)]
            attn_o, attn_l = [a[0] for a in attn], [a[1] for a in attn]
            new_kv = [kv.reshape(BATCH, SEQ, 2, A_HEADS, A_DH)[:, SEQ - min(w, SEQ):] for kv, (w, _) in
                      zip(kvs, ATTN_GROUPS)]
            nb = BATCH
        else:
            m0 = jnp.broadcast_to(state_mlstm_m[layer][:, :, None], (DEC_BATCH, M_HEADS, LANES))
            hm, c_new, n_new, m_new, hist = _mlstm(
                qk, v, o, ifg, conv_w[layer], conv_b[layer], mh_norm_w[layer],
                _pad_rows(state_qk_conv[layer], SUBLANES, True), state_mlstm_C[layer],
                _pad_rows(state_mlstm_n[layer], SUBLANES, False), _pad_rows(m0, SUBLANES, False),
                n_seq=DEC_BATCH, n_chunks=1, chunk=SUBLANES, valid=DEC_SEQ)
            caches = (cache_kv_w128[layer], cache_kv_w512[layer], cache_kv_w2048[layer])
            outs = _attn_sample(qs, kvs, caches)
            attn_o, attn_l = outs[0::2], outs[1::2]
            new_kv = [kv.reshape(DEC_BATCH, DEC_SEQ, 2, A_HEADS, A_DH) for kv in kvs]
            nb = DEC_BATCH
        x1, h2, e_idx, e_gate, e_rank, cnt_new = _merge(hm, attn_o, attn_l, gates, x2d, mods[2], mods[4], mods[3],
                                                         merge_w, cnt, per_token, tiles_per_batch)
        results[name] = dict(
            x1=x1, h2=h2, e_idx=e_idx, e_gate=e_gate, e_rank=e_rank, g2=mods[5], per_token=per_token,
            states=(new_kv[0][None], new_kv[1][None], new_kv[2][None], c_new[None],
                    n_new[:, :M_HEADS][None], m_new[:, :M_HEADS, 0][None],
                    hist[:, SUBLANES - (CONV_W - 1):][None]), shape=x.shape)
        cnt = cnt_new
        stage.append(name)

    h2_all = jnp.concatenate([results["p"]["h2"], results["s"]["h2"]], axis=0)
    e_idx = jnp.concatenate([results["p"]["e_idx"], results["s"]["e_idx"]], axis=0)
    e_rank = jnp.concatenate([results["p"]["e_rank"], results["s"]["e_rank"]], axis=0)
    counts = cnt[0, :N_EXPERTS].astype(jnp.int32)
    dest, blk_e, blk_valid, n_rows = _routing(e_idx, e_rank, counts)
    xs = _dispatch(dest, h2_all, n_rows)
    ys = _experts(blk_e, blk_valid, xs, w_up[layer].astype(BF16), b_up[layer], w_down[layer].astype(BF16),
                  b_down[layer])
    ys_out = {}
    off = 0
    for name in ("p", "s"):
        r = results[name]
        n = r["x1"].shape[0]
        ys_out[name] = _combine(dest[off * TOP_K:(off + n) * TOP_K], ys, r["e_gate"], r["x1"],
                                r["g2"], ln2_g[layer], ln2_b[layer], r["per_token"], tiles_per_batch
                                ).reshape(r["shape"])
        off += n
    return (ys_out["p"], ys_out["s"]) + results["p"]["states"] + results["s"]["states"]
```

```python
import functools

import jax
import jax.numpy as jnp
from jax import lax
from jax.experimental import pallas as pl
from jax.experimental.pallas import tpu as pltpu

D_MODEL = 1024
BATCH = 2
SEQ = 8192
DEPTH = 1
DEC_BATCH = 128
DEC_SEQ = 4
PAST_LEN = 2048

M_HEADS = 4
M_DQK = D_MODEL // 8
M_DV = D_MODEL // 4
CONV_W = 4
ATTN_GROUPS = ((128, 1), (512, 4), (2048, 16))
A_HEADS = 4
A_DH = D_MODEL // 16
N_EXPERTS = 32
TOP_K = 4
D_FF = D_MODEL
SWIGLU_LIMIT = 7.0
SWIGLU_ALPHA = 1.702
DN_ALPHA = (2 * DEPTH) ** 0.25
LN_EPS = 1e-5

QK_W = 2 * M_HEADS * M_DQK
MV_W = M_HEADS * M_DV
A_W = A_HEADS * A_DH
OFF_V = QK_W
OFF_I = OFF_V + MV_W
OFF_F = OFF_I + M_HEADS
OFF_O = OFF_F + M_HEADS
OFF_A = OFF_O + MV_W
N_GROUPS = len(ATTN_GROUPS)
P_W = OFF_A + 3 * A_W * N_GROUPS

LANES = 128
SUBLANES = 8
ATTN_BLK = 128
MLSTM_CHUNK = 64
TOK_TILE = 256
MOE_TILE = 256
VMEM_LIMIT = 56 * 1024 * 1024

SEG_QK = 0
SEG_V = SEG_QK + QK_W
SEG_O = SEG_V + MV_W
SEG_A = SEG_O + MV_W
SEG_G = SEG_A + 3 * A_W * N_GROUPS
SEG_IF = SEG_G + 2 * D_MODEL
W1_COLS = SEG_IF + LANES

F32 = jnp.float32
BF16 = jnp.bfloat16
NEG_INF = float("-inf")


def _ln(x):
    mu = jnp.mean(x, axis=-1, keepdims=True)
    xc = x - mu
    var = jnp.mean(xc * xc, axis=-1, keepdims=True)
    return xc * lax.rsqrt(var + LN_EPS)


def _sigmoid(x):
    return 1.0 / (1.0 + jnp.exp(-x))


def _nt_dot(a, b, precision=None):
    return lax.dot_general(a, b, (((1,), (1,)), ((), ())), preferred_element_type=F32, precision=precision)


def _tn_dot(a, b):
    return lax.dot_general(a, b, (((0,), (0,)), ((), ())), preferred_element_type=F32)


ROW_SHAPE = (D_MODEL // LANES, LANES)


def _row_spec(rows):
    return pl.BlockSpec((rows,) + ROW_SHAPE, lambda i, *_: (i, 0, 0))


def _store_rows(ref, x):
    for s in range(ROW_SHAPE[0]):
        ref[:, s, :] = x[:, s * LANES:(s + 1) * LANES]


def _load_rows(ref):
    return jnp.concatenate([ref[:, s, :] for s in range(ROW_SHAPE[0])], axis=-1)


def _ada_kernel(c_ref, w_ref, b_ref, o_ref):
    c = c_ref[...]
    s = c * _sigmoid(c)
    o_ref[...] = jnp.dot(s.astype(BF16), w_ref[...].astype(BF16), preferred_element_type=F32) + b_ref[...]


def _ada(c_all, w_ada, b_ada):
    r = c_all.shape[0]
    n_col = 6 * D_MODEL // D_MODEL
    return pl.pallas_call(
        _ada_kernel,
        out_shape=jax.ShapeDtypeStruct((r, 6 * D_MODEL), F32),
        grid=(n_col,),
        in_specs=[pl.BlockSpec((r, D_MODEL), lambda j: (0, 0)),
                  pl.BlockSpec((D_MODEL, D_MODEL), lambda j: (0, j)),
                  pl.BlockSpec((1, D_MODEL), lambda j: (0, j))],
        out_specs=pl.BlockSpec((r, D_MODEL), lambda j: (0, j)),
        name="ada_mod",
    )(c_all, w_ada, b_ada.reshape(1, -1))


def _inproj_kernel(x_ref, sc_ref, sh_ref, w_ref, b_ref,
                   qk_ref, v_ref, o_ref, q0_ref, kv0_ref, q1_ref, kv1_ref, q2_ref, kv2_ref, g_ref, if_ref):
    h = _ln(x_ref[...]) * (1.0 + sc_ref[...]) + sh_ref[...]
    hb = h.astype(BF16)

    def seg(a, width):
        return jnp.dot(hb, w_ref[:, a:a + width], preferred_element_type=F32) + b_ref[:, a:a + width]

    qk_ref[...] = seg(SEG_QK, QK_W)
    v_ref[...] = seg(SEG_V, MV_W)
    o_ref[...] = seg(SEG_O, MV_W)
    for gi, (q_ref, kv_ref) in enumerate(((q0_ref, kv0_ref), (q1_ref, kv1_ref), (q2_ref, kv2_ref))):
        base = SEG_A + 3 * A_W * gi
        q_ref[...] = seg(base, A_W)
        kv_ref[...] = seg(base + A_W, 2 * A_W)
    g_ref[...] = _sigmoid(seg(SEG_G, 2 * D_MODEL))
    if_ref[...] = seg(SEG_IF, LANES)


def _mod_spec(per_token, tiles_per_batch):
    if per_token:
        return pl.BlockSpec((TOK_TILE, D_MODEL), lambda i: (i, 0))
    return pl.BlockSpec((None, 1, D_MODEL), lambda i: (i // tiles_per_batch, 0, 0))


def _inproj(x2d, sc, sh, w1, b1, per_token, tiles_per_batch):
    n = x2d.shape[0]
    tok = lambda width: pl.BlockSpec((TOK_TILE, width), lambda i: (i, 0))
    widths = (QK_W, MV_W, MV_W, A_W, 2 * A_W, A_W, 2 * A_W, A_W, 2 * A_W, 2 * D_MODEL, LANES)
    return pl.pallas_call(
        _inproj_kernel,
        out_shape=[jax.ShapeDtypeStruct((n, w), F32) for w in widths],
        grid=(n // TOK_TILE,),
        in_specs=[tok(D_MODEL), _mod_spec(per_token, tiles_per_batch), _mod_spec(per_token, tiles_per_batch),
                  pl.BlockSpec((D_MODEL, W1_COLS), lambda i: (0, 0), pipeline_mode=pl.Buffered(1)),
                  pl.BlockSpec((1, W1_COLS), lambda i: (0, 0))],
        out_specs=[tok(w) for w in widths],
        compiler_params=pltpu.CompilerParams(vmem_limit_bytes=VMEM_LIMIT),
        name="inproj",
    )(x2d, sc, sh, w1, b1)


def _mlstm_kernel(qk_ref, v_ref, o_ref, if_ref, cw_ref, cb_ref, nw_ref, hist0_ref, c0_ref, n0_ref, m0_ref,
                  hm_ref, cout_ref, nout_ref, mout_ref, hout_ref,
                  xbuf, vbuf, obuf, gbuf, c_s, n_s, m_s, *, chunk, valid):
    c_idx = pl.program_id(1)
    last = pl.num_programs(1) - 1

    @pl.when(c_idx == 0)
    def _():
        xbuf[0:SUBLANES, :] = hist0_ref[...]
        c_s[...] = c0_ref[...]
        n_s[...] = n0_ref[...]
        m_s[...] = m0_ref[...]

    if valid < chunk:
        xbuf[SUBLANES:SUBLANES + chunk, :] = jnp.zeros((chunk, QK_W), F32)
        vbuf[...] = jnp.zeros((chunk, MV_W), F32)
        obuf[...] = jnp.zeros((chunk, MV_W), F32)
        gbuf[...] = jnp.zeros((chunk, LANES), F32)
        xbuf[SUBLANES:SUBLANES + valid, :] = qk_ref[...]
        vbuf[0:valid, :] = v_ref[...]
        obuf[0:valid, :] = o_ref[...]
        gbuf[0:valid, :] = if_ref[...]
        v_all, o_all, g_raw = vbuf[...], obuf[...], gbuf[...]
    else:
        xbuf[SUBLANES:SUBLANES + chunk, :] = qk_ref[...]
        v_all, o_all, g_raw = v_ref[...], o_ref[...], if_ref[...]

    conv = cb_ref[...]
    for j in range(CONV_W):
        off = SUBLANES - (CONV_W - 1) + j
        conv = conv + cw_ref[j:j + 1, :] * xbuf[off:off + chunk, :]
    new_hist = xbuf[valid:valid + SUBLANES, :]
    xbuf[0:SUBLANES, :] = new_hist
    act = conv * _sigmoid(conv)
    if valid < chunk:
        rows = lax.broadcasted_iota(jnp.int32, (chunk, 1), 0)
        act = jnp.where(rows < valid, act, 0.0)

    lane = lax.broadcasted_iota(jnp.int32, (chunk, LANES), 1)
    is_f = (lane >= M_HEADS) & (lane < 2 * M_HEADS)
    lf = jnp.minimum(g_raw, 0.0) - jnp.log(1.0 + jnp.exp(-jnp.abs(g_raw)))
    g_lin = jnp.where(is_f, lf, g_raw)
    if valid < chunk:
        rows_l = lax.broadcasted_iota(jnp.int32, (chunk, LANES), 0)
        g_lin = jnp.where((rows_l >= valid) & is_f, 0.0, g_lin)
    ri = lax.broadcasted_iota(jnp.int32, (chunk, chunk), 0)
    ci = lax.broadcasted_iota(jnp.int32, (chunk, chunk), 1)
    causal = ci <= ri
    tri = causal.astype(F32)
    csum = jnp.dot(tri, g_lin, preferred_element_type=F32, precision=lax.Precision.HIGHEST)
    m_col = jnp.where(is_f, csum, g_lin)
    sel = (lax.broadcasted_iota(jnp.int32, (SUBLANES, LANES), 0)
           == lax.broadcasted_iota(jnp.int32, (SUBLANES, LANES), 1)).astype(F32)
    m_row = _nt_dot(sel, m_col, precision=lax.Precision.HIGHEST)
    if valid < chunk:
        tcol = lax.broadcasted_iota(jnp.int32, (chunk, 1), 0)
        trow = lax.broadcasted_iota(jnp.int32, (1, chunk), 1)

    for h in range(M_HEADS):
        q = act[:, h * M_DQK:(h + 1) * M_DQK]
        k = act[:, QK_W // 2 + h * M_DQK:QK_W // 2 + (h + 1) * M_DQK] * (M_DQK ** -0.5)
        v = v_all[:, h * M_DV:(h + 1) * M_DV]
        qb, kb, vb = q.astype(BF16), k.astype(BF16), v.astype(BF16)
        li_col = m_col[:, h:h + 1]
        b_col = m_col[:, M_HEADS + h:M_HEADS + h + 1]
        li_row = m_row[h:h + 1, :]
        b_row = m_row[M_HEADS + h:M_HEADS + h + 1, :]
        if valid < chunk:
            li_col = jnp.where(tcol < valid, li_col, NEG_INF)
            li_row = jnp.where(trow < valid, li_row, NEG_INF)
        m_prev = m_s[h:h + 1, 0:1]
        c_prev = c_s[h]
        n_prev = n_s[h:h + 1, :]

        dmat = jnp.where(causal, b_col - b_row + li_row, NEG_INF)
        inter = b_col + m_prev
        m_t = jnp.maximum(inter, jnp.max(dmat, axis=-1, keepdims=True))
        s_qk = _nt_dot(qb, kb)
        a = jnp.exp(dmat - m_t) * s_qk
        sc = jnp.exp(inter - m_t)
        num = sc * jnp.dot(qb, c_prev.astype(BF16), preferred_element_type=F32) \
            + jnp.dot(a.astype(BF16), vb, preferred_element_type=F32)
        den = sc * jnp.sum(q * n_prev, axis=-1, keepdims=True) + jnp.sum(a, axis=-1, keepdims=True)
        hh = num / jnp.maximum(jnp.abs(den), jnp.exp(-m_t))

        b_end = b_col[chunk - 1:chunk, :]
        g_col = b_end - b_col + li_col
        g_row = b_end - b_row + li_row
        m_new = jnp.maximum(b_end + m_prev, jnp.max(g_row, axis=-1, keepdims=True))
        w_col = jnp.exp(g_col - m_new)
        decay = jnp.exp(b_end + m_prev - m_new)
        c_s[h] = decay * c_prev + _tn_dot(kb, (w_col * v).astype(BF16))
        n_s[h:h + 1, :] = decay * n_prev + jnp.sum(w_col * k, axis=0, keepdims=True)
        m_s[h:h + 1, :] = jnp.broadcast_to(m_new, (1, LANES))

        y = _ln(hh) * nw_ref[:, h * M_DV:(h + 1) * M_DV] * _sigmoid(o_all[:, h * M_DV:(h + 1) * M_DV])
        hm_ref[:, h * M_DV:(h + 1) * M_DV] = y[0:valid, :]

    @pl.when(c_idx == last)
    def _():
        cout_ref[...] = c_s[...]
        nout_ref[...] = n_s[...]
        mout_ref[...] = m_s[...]
        hout_ref[...] = xbuf[0:SUBLANES, :]


def _mlstm(qk, v, o, ifg, conv_w, conv_b, norm_w, hist0, c0, n0, m0, *, n_seq, n_chunks, chunk, valid):
    if valid == chunk:
        tok = lambda width: pl.BlockSpec((chunk, width), lambda b, c: (b * n_chunks + c, 0))
        args = (qk, v, o, ifg)
        hm_shape = jax.ShapeDtypeStruct((n_seq * n_chunks * valid, MV_W), F32)
    else:
        tok = lambda width: pl.BlockSpec((None, valid, width), lambda b, c: (b * n_chunks + c, 0, 0))
        args = tuple(a.reshape(n_seq * n_chunks, valid, a.shape[-1]) for a in (qk, v, o, ifg))
        hm_shape = jax.ShapeDtypeStruct((n_seq * n_chunks, valid, MV_W), F32)
    const2 = lambda r, w: pl.BlockSpec((r, w), lambda b, c: (0, 0))
    per_seq = lambda *dims: pl.BlockSpec((None,) + dims, lambda b, c: (b,) + (0,) * len(dims))
    outs = pl.pallas_call(
        functools.partial(_mlstm_kernel, chunk=chunk, valid=valid),
        out_shape=[hm_shape,
                   jax.ShapeDtypeStruct((n_seq, M_HEADS, M_DQK, M_DV), F32),
                   jax.ShapeDtypeStruct((n_seq, SUBLANES, M_DQK), F32),
                   jax.ShapeDtypeStruct((n_seq, SUBLANES, LANES), F32),
                   jax.ShapeDtypeStruct((n_seq, SUBLANES, QK_W), F32)],
        grid=(n_seq, n_chunks),
        in_specs=[tok(QK_W), tok(MV_W), tok(MV_W), tok(LANES),
                  const2(CONV_W, QK_W), const2(1, QK_W), const2(1, MV_W),
                  per_seq(SUBLANES, QK_W), per_seq(M_HEADS, M_DQK, M_DV), per_seq(SUBLANES, M_DQK),
                  per_seq(SUBLANES, LANES)],
        out_specs=[tok(MV_W), per_seq(M_HEADS, M_DQK, M_DV), per_seq(SUBLANES, M_DQK),
                   per_seq(SUBLANES, LANES), per_seq(SUBLANES, QK_W)],
        scratch_shapes=[pltpu.VMEM((chunk + 2 * SUBLANES, QK_W), F32), pltpu.VMEM((chunk, MV_W), F32),
                        pltpu.VMEM((chunk, MV_W), F32), pltpu.VMEM((chunk, LANES), F32),
                        pltpu.VMEM((M_HEADS, M_DQK, M_DV), F32), pltpu.VMEM((SUBLANES, M_DQK), F32),
                        pltpu.VMEM((SUBLANES, LANES), F32)],
        compiler_params=pltpu.CompilerParams(dimension_semantics=("arbitrary", "arbitrary"),
                                             vmem_limit_bytes=VMEM_LIMIT),
        name="mlstm",
    )(*args, conv_w, conv_b.reshape(1, -1), norm_w.reshape(1, -1), hist0, c0, n0, m0)
    hm = outs[0].reshape(n_seq * n_chunks * valid, MV_W)
    return hm, outs[1], outs[2], outs[3], outs[4]


def _pad_rows(a, rows, at_end):
    pad = rows - a.shape[1]
    cfg = ((0, 0), (pad, 0), (0, 0)) if at_end else ((0, 0), (0, pad), (0, 0))
    return jnp.pad(a, cfg)


def _attn_prompt_kernel(q_ref, kvc_ref, kvp_ref, o_ref, l_ref):
    n = pl.program_id(2)
    qi = lax.broadcasted_iota(jnp.int32, (ATTN_BLK, ATTN_BLK), 0)
    ki = lax.broadcasted_iota(jnp.int32, (ATTN_BLK, ATTN_BLK), 1)
    mask_c = ki <= qi
    mask_p = (ki >= qi) & (n > 0)
    q = q_ref[...] * (A_DH ** -0.5)
    kvc = kvc_ref[...]
    kvp = kvp_ref[...]
    outs, lses = [], []
    for h in range(A_HEADS):
        sl = slice(h * A_DH, (h + 1) * A_DH)
        slv = slice(A_W + h * A_DH, A_W + (h + 1) * A_DH)
        qh = q[:, sl].astype(BF16)
        s_c = jnp.where(mask_c, _nt_dot(qh, kvc[:, sl].astype(BF16)), NEG_INF)
        s_p = jnp.where(mask_p, _nt_dot(qh, kvp[:, sl].astype(BF16)), NEG_INF)
        mx = jnp.maximum(jnp.max(s_c, axis=-1, keepdims=True), jnp.max(s_p, axis=-1, keepdims=True))
        p_c = jnp.exp(s_c - mx)
        p_p = jnp.exp(s_p - mx)
        l = jnp.sum(p_c, axis=-1, keepdims=True) + jnp.sum(p_p, axis=-1, keepdims=True)
        inv = 1.0 / l
        pv = jnp.dot((p_c * inv).astype(BF16), kvc[:, slv].astype(BF16), preferred_element_type=F32) \
            + jnp.dot((p_p * inv).astype(BF16), kvp[:, slv].astype(BF16), preferred_element_type=F32)
        outs.append(pv)
        lses.append(jnp.broadcast_to(mx + jnp.log(l), (ATTN_BLK, A_DH)))
    o_ref[...] = jnp.concatenate(outs, axis=-1)
    l_ref[...] = jnp.concatenate(lses, axis=-1)


def _attn_prompt(q, kv, dilation):
    t_d = SEQ // dilation
    nb = t_d // ATTN_BLK
    q3 = q.reshape(BATCH, t_d, dilation * A_W)
    kv3 = kv.reshape(BATCH, t_d, dilation * 2 * A_W)
    qspec = pl.BlockSpec((None, ATTN_BLK, A_W), lambda b, r, n: (b, n, r))
    o, l = pl.pallas_call(
        _attn_prompt_kernel,
        out_shape=[jax.ShapeDtypeStruct(q3.shape, F32)] * 2,
        grid=(BATCH, dilation, nb),
        in_specs=[qspec,
                  pl.BlockSpec((None, ATTN_BLK, 2 * A_W), lambda b, r, n: (b, n, r)),
                  pl.BlockSpec((None, ATTN_BLK, 2 * A_W), lambda b, r, n: (b, jnp.maximum(n - 1, 0), r))],
        out_specs=[qspec, qspec],
        name="attn_prompt",
    )(q3, kv3, kv3)
    return o.reshape(BATCH * SEQ, A_W), l.reshape(BATCH * SEQ, A_W)


def _attn_sample_group(qpad, kvpad, cache_ref, window, dilation):
    cache_len = cache_ref.shape[1]
    tok = lax.broadcasted_iota(jnp.int32, (SUBLANES, 1), 0)
    delta = cache_len + tok - lax.broadcasted_iota(jnp.int32, (1, cache_len), 1)
    ok_c = ((delta & (dilation - 1)) == 0) & (delta >= dilation) & (delta <= window)
    npos = lax.broadcasted_iota(jnp.int32, (1, SUBLANES), 1)
    dnew = tok - npos
    ok_n = (dnew >= 0) & ((dnew & (dilation - 1)) == 0) & (dnew <= window) & (npos < DEC_SEQ)
    outs, lses = [], []
    for h in range(A_HEADS):
        sl = slice(h * A_DH, (h + 1) * A_DH)
        slv = slice(A_W + h * A_DH, A_W + (h + 1) * A_DH)
        qh = (qpad[:, sl] * (A_DH ** -0.5)).astype(BF16)
        kt = cache_ref[h * A_DH:(h + 1) * A_DH, :].astype(BF16)
        vt = cache_ref[A_W + h * A_DH:A_W + (h + 1) * A_DH, :].astype(BF16)
        s_c = jnp.where(ok_c, jnp.dot(qh, kt, preferred_element_type=F32), NEG_INF)
        s_n = jnp.where(ok_n, _nt_dot(qh, kvpad[:, sl].astype(BF16)), NEG_INF)
        mx = jnp.maximum(jnp.max(s_c, axis=-1, keepdims=True), jnp.max(s_n, axis=-1, keepdims=True))
        p_c = jnp.exp(s_c - mx)
        p_n = jnp.exp(s_n - mx)
        l = jnp.sum(p_c, axis=-1, keepdims=True) + jnp.sum(p_n, axis=-1, keepdims=True)
        inv = 1.0 / l
        o = _nt_dot((p_c * inv).astype(BF16), vt) \
            + jnp.dot((p_n * inv).astype(BF16), kvpad[:, slv].astype(BF16), preferred_element_type=F32)
        outs.append(o)
        lses.append(jnp.broadcast_to(mx + jnp.log(l), (SUBLANES, A_DH)))
    return jnp.concatenate(outs, axis=-1), jnp.concatenate(lses, axis=-1)


def _attn_sample_kernel(q0_ref, kv0_ref, c0_ref, q1_ref, kv1_ref, c1_ref, q2_ref, kv2_ref, c2_ref,
                        o0_ref, l0_ref, o1_ref, l1_ref, o2_ref, l2_ref, qpad, kvpad):
    ins = ((q0_ref, kv0_ref, c0_ref, o0_ref, l0_ref), (q1_ref, kv1_ref, c1_ref, o1_ref, l1_ref),
           (q2_ref, kv2_ref, c2_ref, o2_ref, l2_ref))
    qpad[...] = jnp.zeros((SUBLANES, A_W), F32)
    kvpad[...] = jnp.zeros((SUBLANES, 2 * A_W), F32)
    for (window, dilation), (q_ref, kv_ref, c_ref, o_ref, l_ref) in zip(ATTN_GROUPS, ins):
        qpad[0:DEC_SEQ, :] = q_ref[...]
        kvpad[0:DEC_SEQ, :] = kv_ref[...]
        o, le = _attn_sample_group(qpad[...], kvpad[...], c_ref, window, dilation)
        o_ref[...] = o[0:DEC_SEQ, :]
        l_ref[...] = le[0:DEC_SEQ, :]


def _attn_sample(qs, kvs, caches):
    args, in_specs = [], []
    tok = lambda width: pl.BlockSpec((None, DEC_SEQ, width), lambda b: (b, 0, 0))
    for q, kv, cache in zip(qs, kvs, caches):
        cache_len = cache.shape[1]
        c3 = jnp.transpose(cache, (0, 2, 3, 4, 1)).reshape(DEC_BATCH, 2 * A_W, cache_len)
        args += [q.reshape(DEC_BATCH, DEC_SEQ, A_W), kv.reshape(DEC_BATCH, DEC_SEQ, 2 * A_W), c3]
        in_specs += [tok(A_W), tok(2 * A_W), pl.BlockSpec((None, 2 * A_W, cache_len), lambda b: (b, 0, 0))]
    outs = pl.pallas_call(
        _attn_sample_kernel,
        out_shape=[jax.ShapeDtypeStruct((DEC_BATCH, DEC_SEQ, A_W), F32)] * (2 * N_GROUPS),
        grid=(DEC_BATCH,),
        in_specs=in_specs,
        out_specs=[tok(A_W)] * (2 * N_GROUPS),
        scratch_shapes=[pltpu.VMEM((SUBLANES, A_W), F32), pltpu.VMEM((SUBLANES, 2 * A_W), F32)],
        compiler_params=pltpu.CompilerParams(vmem_limit_bytes=VMEM_LIMIT),
        name="attn_sample",
    )(*args)
    return [a.reshape(DEC_BATCH * DEC_SEQ, A_W) for a in outs]


def _merge_kernel(hm_ref, o0_ref, l0_ref, o1_ref, l1_ref, o2_ref, l2_ref, g_ref, x_ref, g1_ref, sc2_ref, sh2_ref,
                  wbm_ref, wba_ref, wo_ref, lng_ref, lnb_ref, wr_ref, br_ref, cnt0_ref,
                  x1_ref, h2_ref, ei_ref, eg_ref, er_ref, cnt_ref, run_s):
    i = pl.program_id(0)

    @pl.when(i == 0)
    def _():
        run_s[...] = cnt0_ref[...]

    l0, l1, l2 = l0_ref[...], l1_ref[...], l2_ref[...]
    mx = jnp.maximum(jnp.maximum(l0, l1), l2)
    e0, e1, e2 = jnp.exp(l0 - mx), jnp.exp(l1 - mx), jnp.exp(l2 - mx)
    inv = 1.0 / (e0 + e1 + e2)
    attn = (e0 * inv) * o0_ref[...] + (e1 * inv) * o1_ref[...] + (e2 * inv) * o2_ref[...]
    br_m = jnp.dot(hm_ref[...].astype(BF16), wbm_ref[...], preferred_element_type=F32)
    br_a = jnp.dot(attn.astype(BF16), wba_ref[...], preferred_element_type=F32)
    merged = g_ref[:, 0:D_MODEL] * br_m + g_ref[:, D_MODEL:2 * D_MODEL] * br_a
    mix = jnp.dot(merged.astype(BF16), wo_ref[...], preferred_element_type=F32)
    x1 = _ln(DN_ALPHA * x_ref[...] + g1_ref[...] * mix) * lng_ref[...] + lnb_ref[...]
    x1_ref[...] = x1
    h2 = _ln(x1) * (1.0 + sc2_ref[...]) + sh2_ref[...]
    _store_rows(h2_ref, h2)

    lane = lax.broadcasted_iota(jnp.int32, (TOK_TILE, LANES), 1)
    lane_f = lane.astype(F32)
    logits = jnp.dot(h2.astype(BF16), wr_ref[...], preferred_element_type=F32) + br_ref[...]
    cur = jnp.where(lane < N_EXPERTS, logits, NEG_INF)
    vals, onehots = [], []
    idx_mat = jnp.zeros((TOK_TILE, LANES), F32)
    for k in range(TOP_K):
        m = jnp.max(cur, axis=-1, keepdims=True)
        sel = jnp.min(jnp.where(cur == m, lane_f, float(LANES)), axis=-1, keepdims=True)
        hit = lane_f == sel
        vals.append(m)
        onehots.append(hit)
        idx_mat = jnp.where(lane == k, sel, idx_mat)
        cur = jnp.where(hit, NEG_INF, cur)
    es = [jnp.exp(v - vals[0]) for v in vals]
    tot = es[0] + es[1] + es[2] + es[3]
    gate_mat = jnp.zeros((TOK_TILE, LANES), F32)
    for k in range(TOP_K):
        gate_mat = jnp.where(lane == k, es[k] / tot, gate_mat)

    cnt = jnp.zeros((TOK_TILE, LANES), F32)
    for hit in onehots:
        cnt = cnt + hit.astype(F32)
    ri = lax.broadcasted_iota(jnp.int32, (TOK_TILE, TOK_TILE), 0)
    ci = lax.broadcasted_iota(jnp.int32, (TOK_TILE, TOK_TILE), 1)
    before = (ci < ri).astype(BF16)
    base = run_s[0:1, :] + jnp.dot(before, cnt.astype(BF16), preferred_element_type=F32)
    rank_mat = jnp.zeros((TOK_TILE, LANES), F32)
    for k, hit in enumerate(onehots):
        rk = jnp.sum(jnp.where(hit, base, 0.0), axis=-1, keepdims=True)
        rank_mat = jnp.where(lane == k, rk, rank_mat)
    new_run = run_s[0:1, :] + jnp.sum(cnt, axis=0, keepdims=True)
    run_s[...] = jnp.broadcast_to(new_run, (SUBLANES, LANES))
    cnt_ref[...] = run_s[...]
    ei_ref[...] = idx_mat[:, 0:TOP_K].astype(jnp.int32)
    eg_ref[...] = gate_mat[:, 0:TOP_K]
    er_ref[...] = rank_mat[:, 0:TOP_K].astype(jnp.int32)


def _merge(hm, attn_o, attn_l, gates, x2d, g1, sc2, sh2, wts, cnt0, per_token, tiles_per_batch):
    n = x2d.shape[0]
    tok = lambda width: pl.BlockSpec((TOK_TILE, width), lambda i: (i, 0))
    const = lambda a: pl.BlockSpec(a.shape, lambda i: (0,) * a.ndim)
    mod = _mod_spec(per_token, tiles_per_batch)
    attn_args, attn_specs = [], []
    for o, l in zip(attn_o, attn_l):
        attn_args += [o, l]
        attn_specs += [tok(A_W), tok(A_W)]
    return pl.pallas_call(
        _merge_kernel,
        out_shape=[jax.ShapeDtypeStruct((n, D_MODEL), F32), jax.ShapeDtypeStruct((n,) + ROW_SHAPE, F32),
                   jax.ShapeDtypeStruct((n, TOP_K), jnp.int32), jax.ShapeDtypeStruct((n, TOP_K), F32),
                   jax.ShapeDtypeStruct((n, TOP_K), jnp.int32), jax.ShapeDtypeStruct((SUBLANES, LANES), F32)],
        grid=(n // TOK_TILE,),
        in_specs=[tok(MV_W)] + attn_specs + [tok(2 * D_MODEL), tok(D_MODEL), mod, mod, mod]
                 + [const(w) for w in wts] + [const(cnt0)],
        out_specs=[tok(D_MODEL), _row_spec(TOK_TILE), tok(TOP_K), tok(TOP_K), tok(TOP_K),
                   pl.BlockSpec((SUBLANES, LANES), lambda i: (0, 0))],
        scratch_shapes=[pltpu.VMEM((SUBLANES, LANES), F32)],
        compiler_params=pltpu.CompilerParams(dimension_semantics=("arbitrary",), vmem_limit_bytes=VMEM_LIMIT),
        name="merge_router",
    )(hm, *attn_args, gates, x2d, g1, sc2, sh2, *wts, cnt0)


def _row_copy(src_hbm, src_row, dst, dst_row, sem):
    return pltpu.make_async_copy(src_hbm.at[pl.ds(src_row, 1)], dst.at[pl.ds(dst_row, 1)], sem)


def _dispatch_kernel(dest_ref, h2_ref, xs_in_hbm, xs_hbm, sem):
    del xs_in_hbm

    def issue(j, carry):
        _row_copy(h2_ref, j // TOP_K, xs_hbm, dest_ref[j], sem).start()
        return carry

    def drain(j, carry):
        _row_copy(h2_ref, 0, xs_hbm, 0, sem).wait()
        return carry

    lax.fori_loop(0, TOK_TILE * TOP_K, issue, 0)
    lax.fori_loop(0, TOK_TILE * TOP_K, drain, 0)


def _dispatch(dest_flat, h2, n_rows):
    n = h2.shape[0]
    xs0 = jnp.zeros((n_rows,) + ROW_SHAPE, F32)
    return pl.pallas_call(
        _dispatch_kernel,
        out_shape=jax.ShapeDtypeStruct((n_rows,) + ROW_SHAPE, F32),
        grid=(n // TOK_TILE,),
        in_specs=[pl.BlockSpec((TOK_TILE * TOP_K,), lambda i: (i,), memory_space=pltpu.SMEM),
                  _row_spec(TOK_TILE), pl.BlockSpec(memory_space=pl.ANY)],
        out_specs=pl.BlockSpec(memory_space=pl.ANY),
        scratch_shapes=[pltpu.SemaphoreType.DMA(())],
        input_output_aliases={2: 0},
        compiler_params=pltpu.CompilerParams(dimension_semantics=("arbitrary",)),
        name="moe_dispatch",
    )(dest_flat, h2, xs0)


def _expert_kernel(be_ref, bv_ref, xs_ref, wu_ref, bu_ref, wd_ref, bd_ref, ys_ref):
    i = pl.program_id(0)

    @pl.when(bv_ref[i] == 1)
    def _():
        hu = jnp.dot(_load_rows(xs_ref).astype(BF16), wu_ref[...], preferred_element_type=F32) + bu_ref[...]
        x_glu = jnp.minimum(hu[:, 0:D_FF], SWIGLU_LIMIT)
        x_lin = jnp.clip(hu[:, D_FF:2 * D_FF], -SWIGLU_LIMIT, SWIGLU_LIMIT)
        act = x_glu * _sigmoid(SWIGLU_ALPHA * x_glu) * (x_lin + 1.0)
        _store_rows(ys_ref, jnp.dot(act.astype(BF16), wd_ref[...], preferred_element_type=F32) + bd_ref[...])

    @pl.when(bv_ref[i] == 0)
    def _():
        ys_ref[...] = jnp.zeros((MOE_TILE,) + ROW_SHAPE, F32)


def _experts(blk_e, blk_valid, xs, w_up, b_up, w_down, b_down):
    n_blk = blk_e.shape[0]
    return pl.pallas_call(
        _expert_kernel,
        out_shape=jax.ShapeDtypeStruct(xs.shape, F32),
        grid_spec=pltpu.PrefetchScalarGridSpec(
            num_scalar_prefetch=2,
            grid=(n_blk,),
            in_specs=[_row_spec(MOE_TILE),
                      pl.BlockSpec((None, D_MODEL, 2 * D_FF), lambda i, be, bv: (be[i], 0, 0)),
                      pl.BlockSpec((None, 1, 2 * D_FF), lambda i, be, bv: (be[i], 0, 0)),
                      pl.BlockSpec((None, D_FF, D_MODEL), lambda i, be, bv: (be[i], 0, 0)),
                      pl.BlockSpec((None, 1, D_MODEL), lambda i, be, bv: (be[i], 0, 0))],
            out_specs=_row_spec(MOE_TILE)),
        compiler_params=pltpu.CompilerParams(dimension_semantics=("arbitrary",), vmem_limit_bytes=VMEM_LIMIT),
        name="moe_experts",
    )(blk_e, blk_valid, xs, w_up, b_up.reshape(N_EXPERTS, 1, -1), w_down, b_down.reshape(N_EXPERTS, 1, -1))


def _combine_kernel(dest_ref, ys_hbm, gate_ref, x1_ref, g2_ref, lng_ref, lnb_ref, out_ref, rows, sem):
    def issue(j, carry):
        _row_copy(ys_hbm, dest_ref[j], rows.at[j % TOP_K], j // TOP_K, sem).start()
        return carry

    def drain(j, carry):
        _row_copy(ys_hbm, 0, rows.at[0], 0, sem).wait()
        return carry

    lax.fori_loop(0, TOK_TILE * TOP_K, issue, 0)
    lax.fori_loop(0, TOK_TILE * TOP_K, drain, 0)
    gate = gate_ref[...]
    f = gate[:, 0:1] * _load_rows(rows.at[0])
    for k in range(1, TOP_K):
        f = f + gate[:, k:k + 1] * _load_rows(rows.at[k])
    out_ref[...] = _ln(DN_ALPHA * x1_ref[...] + g2_ref[...] * f) * lng_ref[...] + lnb_ref[...]


def _combine(dest_flat, ys, gate, x1, g2, ln_g, ln_b, per_token, tiles_per_batch):
    n = x1.shape[0]
    tok = lambda width: pl.BlockSpec((TOK_TILE, width), lambda i: (i, 0))
    const = pl.BlockSpec((1, D_MODEL), lambda i: (0, 0))
    return pl.pallas_call(
        _combine_kernel,
        out_shape=jax.ShapeDtypeStruct((n, D_MODEL), F32),
        grid=(n // TOK_TILE,),
        in_specs=[pl.BlockSpec((TOK_TILE * TOP_K,), lambda i: (i,), memory_space=pltpu.SMEM),
                  pl.BlockSpec(memory_space=pl.ANY), tok(TOP_K), tok(D_MODEL),
                  _mod_spec(per_token, tiles_per_batch), const, const],
        out_specs=tok(D_MODEL),
        scratch_shapes=[pltpu.VMEM((TOP_K, TOK_TILE) + ROW_SHAPE, F32), pltpu.SemaphoreType.DMA(())],
        compiler_params=pltpu.CompilerParams(dimension_semantics=("arbitrary",), vmem_limit_bytes=VMEM_LIMIT),
        name="moe_combine",
    )(dest_flat, ys, gate, x1, g2, ln_g.reshape(1, -1), ln_b.reshape(1, -1))


def _routing(e_idx, rank, counts):
    n_assign = e_idx.shape[0] * TOP_K
    n_blk = n_assign // MOE_TILE + N_EXPERTS
    padded = (counts + MOE_TILE - 1) // MOE_TILE * MOE_TILE
    pend = jnp.cumsum(padded)
    pstart = pend - padded
    dest = pstart[e_idx] + rank
    blk_row = jnp.arange(n_blk, dtype=jnp.int32) * MOE_TILE
    blk_e = jnp.minimum(jnp.sum((pend[None, :] <= blk_row[:, None]).astype(jnp.int32), axis=1), N_EXPERTS - 1)
    blk_valid = (blk_row < pend[-1]).astype(jnp.int32)
    return dest.reshape(-1).astype(jnp.int32), blk_e, blk_valid, n_blk * MOE_TILE


def kernel(x_prompt, x_sample, c_prompt, c_sample, cache_kv_w128, cache_kv_w512, cache_kv_w2048,
           state_mlstm_C, state_mlstm_n, state_mlstm_m, state_qk_conv,
           w_ada, b_ada, w_in, b_in, conv_w, conv_b, mh_norm_w, w_br_mlstm, w_br_attn,
           w_merge, b_merge, w_o, ln1_g, ln1_b, w_router, b_router, w_up, b_up, w_down, b_down,
           ln2_g, ln2_b):
    assert DEPTH == 1 and SEQ % (ATTN_GROUPS[-1][0]) == 0 and SEQ % MLSTM_CHUNK == 0
    n_p, n_s = BATCH * SEQ, DEC_BATCH * DEC_SEQ
    tiles_per_batch = SEQ // TOK_TILE
    layer = 0

    wi, bi = w_in[layer], b_in[layer]
    pad_if = LANES - 2 * M_HEADS
    w1 = jnp.concatenate([wi[:, :OFF_V], wi[:, OFF_V:OFF_I], wi[:, OFF_O:OFF_A], wi[:, OFF_A:], w_merge[layer],
                          wi[:, OFF_I:OFF_O], jnp.zeros((D_MODEL, pad_if), F32)], axis=1).astype(BF16)
    b1 = jnp.concatenate([bi[:OFF_V], bi[OFF_V:OFF_I], bi[OFF_O:OFF_A], bi[OFF_A:], b_merge[layer],
                          bi[OFF_I:OFF_O], jnp.zeros((pad_if,), F32)]).reshape(1, -1)
    wr = jnp.pad(w_router[layer], ((0, 0), (0, LANES - N_EXPERTS))).astype(BF16)
    br = jnp.pad(b_router[layer], (0, LANES - N_EXPERTS)).reshape(1, -1)
    merge_w = (w_br_mlstm[layer].astype(BF16), w_br_attn[layer].astype(BF16), w_o[layer].astype(BF16),
               ln1_g[layer].reshape(1, -1), ln1_b[layer].reshape(1, -1), wr, br)

    mod = _ada(jnp.concatenate([c_prompt, c_sample], axis=0), w_ada[layer], b_ada[layer])
    mod_p = mod[:BATCH].reshape(BATCH, 6, 1, D_MODEL)
    mod_s = jnp.repeat(mod[BATCH:], DEC_SEQ, axis=0).reshape(n_s, 6, D_MODEL)
    mods_p = [mod_p[:, j] for j in range(6)]
    mods_s = [mod_s[:, j] for j in range(6)]

    results = {}
    cnt = jnp.zeros((SUBLANES, LANES), F32)
    stage = []
    for name, x, mods, per_token in (("p", x_prompt, mods_p, False), ("s", x_sample, mods_s, True)):
        x2d = x.reshape(-1, D_MODEL)
        qk, v, o, q0, kv0, q1, kv1, q2, kv2, gates, ifg = _inproj(x2d, mods[1], mods[0], w1, b1, per_token,
                                                                    tiles_per_batch)
        qs, kvs = (q0, q1, q2), (kv0, kv1, kv2)
        if name == "p":
            zeros = lambda *s: jnp.zeros(s, F32)
            hm, c_new, n_new, m_new, hist = _mlstm(
                qk, v, o, ifg, conv_w[layer], conv_b[layer], mh_norm_w[layer],
                zeros(BATCH, SUBLANES, QK_W), zeros(BATCH, M_HEADS, M_DQK, M_DV), zeros(BATCH, SUBLANES, M_DQK),
                zeros(BATCH, SUBLANES, LANES), n_seq=BATCH, n_chunks=SEQ // MLSTM_CHUNK, chunk=MLSTM_CHUNK,
                valid=MLSTM_CHUNK)
            attn = [_attn_prompt(q, kv, dil) for q, kv, (_, dil) in zip(qs, kvs, ATTN_GROUPS)]
            attn_o, attn_l = [a[0] for a in attn], [a[1] for a in attn]
            new_kv = [kv.reshape(BATCH, SEQ, 2 * A_W)[:, SEQ - min(w, SEQ):].reshape(BATCH, -1, 2, A_HEADS, A_DH)
                      for kv, (w, _) in zip(kvs, ATTN_GROUPS)]
            nb = BATCH
        else:
            m0 = jnp.broadcast_to(state_mlstm_m[layer][:, :, None], (DEC_BATCH, M_HEADS, LANES))
            hm, c_new, n_new, m_new, hist = _mlstm(
                qk, v, o, ifg, conv_w[layer], conv_b[layer], mh_norm_w[layer],
                _pad_rows(state_qk_conv[layer], SUBLANES, True), state_mlstm_C[layer],
                _pad_rows(state_mlstm_n[layer], SUBLANES, False), _pad_rows(m0, SUBLANES, False),
                n_seq=DEC_BATCH, n_chunks=1, chunk=SUBLANES, valid=DEC_SEQ)
            caches = (cache_kv_w128[layer], cache_kv_w512[layer], cache_kv_w2048[layer])
            outs = _attn_sample(qs, kvs, caches)
            attn_o, attn_l = outs[0::2], outs[1::2]
            new_kv = [kv.reshape(DEC_BATCH, DEC_SEQ, 2, A_HEADS, A_DH) for kv in kvs]
            nb = DEC_BATCH
        x1, h2, e_idx, e_gate, e_rank, cnt_new = _merge(hm, attn_o, attn_l, gates, x2d, mods[2], mods[4], mods[3],
                                                         merge_w, cnt, per_token, tiles_per_batch)
        results[name] = dict(
            x1=x1, h2=h2, e_idx=e_idx, e_gate=e_gate, e_rank=e_rank, g2=mods[5], per_token=per_token,
            states=(new_kv[0][None], new_kv[1][None], new_kv[2][None], c_new[None],
                    n_new[:, :M_HEADS][None], m_new[:, :M_HEADS, 0][None],
                    hist[:, SUBLANES - (CONV_W - 1):][None]), shape=x.shape)
        cnt = cnt_new
        stage.append(name)

    h2_all = jnp.concatenate([results["p"]["h2"], results["s"]["h2"]], axis=0)
    e_idx = jnp.concatenate([results["p"]["e_idx"], results["s"]["e_idx"]], axis=0)
    e_rank = jnp.concatenate([results["p"]["e_rank"], results["s"]["e_rank"]], axis=0)
    counts = cnt[0, :N_EXPERTS].astype(jnp.int32)
    dest, blk_e, blk_valid, n_rows = _routing(e_idx, e_rank, counts)
    xs = _dispatch(dest, h2_all, n_rows)
    ys = _experts(blk_e, blk_valid, xs, w_up[layer].astype(BF16), b_up[layer], w_down[layer].astype(BF16),
                  b_down[layer])
    ys_out = {}
    off = 0
    for name in ("p", "s"):
        r = results[name]
        n = r["x1"].shape[0]
        ys_out[name] = _combine(dest[off * TOP_K:(off + n) * TOP_K], ys, r["e_gate"], r["x1"],
                                r["g2"], ln2_g[layer], ln2_b[layer], r["per_token"], tiles_per_batch
                                ).reshape(r["shape"])
        off += n
    return (ys_out["p"], ys_out["s"]) + results["p"]["states"] + results["s"]["states"]
```

```python
import functools

import jax
import jax.numpy as jnp
from jax import lax
from jax.experimental import pallas as pl
from jax.experimental.pallas import tpu as pltpu

D_MODEL = 1024
BATCH = 2
SEQ = 8192
DEPTH = 1
DEC_BATCH = 128
DEC_SEQ = 4
PAST_LEN = 2048

M_HEADS = 4
M_DQK = D_MODEL // 8
M_DV = D_MODEL // 4
CONV_W = 4
ATTN_GROUPS = ((128, 1), (512, 4), (2048, 16))
A_HEADS = 4
A_DH = D_MODEL // 16
N_EXPERTS = 32
TOP_K = 4
D_FF = D_MODEL
SWIGLU_LIMIT = 7.0
SWIGLU_ALPHA = 1.702
DN_ALPHA = (2 * DEPTH) ** 0.25
LN_EPS = 1e-5

QK_W = 2 * M_HEADS * M_DQK
MV_W = M_HEADS * M_DV
A_W = A_HEADS * A_DH
OFF_V = QK_W
OFF_I = OFF_V + MV_W
OFF_F = OFF_I + M_HEADS
OFF_O = OFF_F + M_HEADS
OFF_A = OFF_O + MV_W
N_GROUPS = len(ATTN_GROUPS)
P_W = OFF_A + 3 * A_W * N_GROUPS

LANES = 128
SUBLANES = 8
ATTN_BLK = 128
MLSTM_CHUNK = 128
TOK_TILE = 256
MOE_TILE = 256
VMEM_LIMIT = 56 * 1024 * 1024

SEG_QK = 0
SEG_V = SEG_QK + QK_W
SEG_O = SEG_V + MV_W
SEG_A = SEG_O + MV_W
SEG_G = SEG_A + 3 * A_W * N_GROUPS
SEG_IF = SEG_G + 2 * D_MODEL
W1_COLS = SEG_IF + LANES

F32 = jnp.float32
BF16 = jnp.bfloat16
NEG_INF = float("-inf")


def _ln(x):
    mu = jnp.mean(x, axis=-1, keepdims=True)
    xc = x - mu
    var = jnp.mean(xc * xc, axis=-1, keepdims=True)
    return xc * lax.rsqrt(var + LN_EPS)


def _sigmoid(x):
    return 1.0 / (1.0 + jnp.exp(-x))


def _nt_dot(a, b, precision=None):
    return lax.dot_general(a, b, (((1,), (1,)), ((), ())), preferred_element_type=F32, precision=precision)


def _tn_dot(a, b):
    return lax.dot_general(a, b, (((0,), (0,)), ((), ())), preferred_element_type=F32)


def _ada_kernel(c_ref, w_ref, b_ref, o_ref):
    c = c_ref[...]
    s = c * _sigmoid(c)
    o_ref[...] = jnp.dot(s.astype(BF16), w_ref[...].astype(BF16), preferred_element_type=F32) + b_ref[...]


def _ada(c_all, w_ada, b_ada):
    r = c_all.shape[0]
    n_col = 6 * D_MODEL // D_MODEL
    return pl.pallas_call(
        _ada_kernel,
        out_shape=jax.ShapeDtypeStruct((r, 6 * D_MODEL), F32),
        grid=(n_col,),
        in_specs=[pl.BlockSpec((r, D_MODEL), lambda j: (0, 0)),
                  pl.BlockSpec((D_MODEL, D_MODEL), lambda j: (0, j)),
                  pl.BlockSpec((1, D_MODEL), lambda j: (0, j))],
        out_specs=pl.BlockSpec((r, D_MODEL), lambda j: (0, j)),
        name="ada_mod",
    )(c_all, w_ada, b_ada.reshape(1, -1))


def _inproj_kernel(x_ref, sc_ref, sh_ref, w_ref, b_ref,
                   qk_ref, v_ref, o_ref, q0_ref, kv0_ref, q1_ref, kv1_ref, q2_ref, kv2_ref, g_ref, if_ref):
    h = _ln(x_ref[...]) * (1.0 + sc_ref[...]) + sh_ref[...]
    hb = h.astype(BF16)

    def seg(a, width):
        return jnp.dot(hb, w_ref[:, a:a + width], preferred_element_type=F32) + b_ref[:, a:a + width]

    qk_ref[...] = seg(SEG_QK, QK_W)
    v_ref[...] = seg(SEG_V, MV_W)
    o_ref[...] = seg(SEG_O, MV_W)
    for gi, (q_ref, kv_ref) in enumerate(((q0_ref, kv0_ref), (q1_ref, kv1_ref), (q2_ref, kv2_ref))):
        base = SEG_A + 3 * A_W * gi
        q_ref[...] = seg(base, A_W)
        kv_ref[...] = seg(base + A_W, 2 * A_W)
    g_ref[...] = _sigmoid(seg(SEG_G, 2 * D_MODEL))
    if_ref[...] = seg(SEG_IF, LANES)


def _mod_spec(per_token, tiles_per_batch):
    if per_token:
        return pl.BlockSpec((TOK_TILE, D_MODEL), lambda i: (i, 0))
    return pl.BlockSpec((None, 1, D_MODEL), lambda i: (i // tiles_per_batch, 0, 0))


def _inproj(x2d, sc, sh, w1, b1, per_token, tiles_per_batch):
    n = x2d.shape[0]
    tok = lambda width: pl.BlockSpec((TOK_TILE, width), lambda i: (i, 0))
    widths = (QK_W, MV_W, MV_W, A_W, 2 * A_W, A_W, 2 * A_W, A_W, 2 * A_W, 2 * D_MODEL, LANES)
    return pl.pallas_call(
        _inproj_kernel,
        out_shape=[jax.ShapeDtypeStruct((n, w), F32) for w in widths],
        grid=(n // TOK_TILE,),
        in_specs=[tok(D_MODEL), _mod_spec(per_token, tiles_per_batch), _mod_spec(per_token, tiles_per_batch),
                  pl.BlockSpec((D_MODEL, W1_COLS), lambda i: (0, 0), pipeline_mode=pl.Buffered(1)),
                  pl.BlockSpec((1, W1_COLS), lambda i: (0, 0))],
        out_specs=[tok(w) for w in widths],
        compiler_params=pltpu.CompilerParams(vmem_limit_bytes=VMEM_LIMIT),
        name="inproj",
    )(x2d, sc, sh, w1, b1)


def _mlstm_kernel(qk_ref, v_ref, o_ref, if_ref, cw_ref, cb_ref, nw_ref, hist0_ref, c0_ref, n0_ref, m0_ref,
                  hm_ref, cout_ref, nout_ref, mout_ref, hout_ref,
                  xbuf, vbuf, obuf, gbuf, c_s, n_s, m_s, *, chunk, valid):
    c_idx = pl.program_id(1)
    last = pl.num_programs(1) - 1

    @pl.when(c_idx == 0)
    def _():
        xbuf[0:SUBLANES, :] = hist0_ref[...]
        c_s[...] = c0_ref[...]
        n_s[...] = n0_ref[...]
        m_s[...] = m0_ref[...]

    if valid < chunk:
        xbuf[SUBLANES:SUBLANES + chunk, :] = jnp.zeros((chunk, QK_W), F32)
        vbuf[...] = jnp.zeros((chunk, MV_W), F32)
        obuf[...] = jnp.zeros((chunk, MV_W), F32)
        gbuf[...] = jnp.zeros((chunk, LANES), F32)
        xbuf[SUBLANES:SUBLANES + valid, :] = qk_ref[...]
        vbuf[0:valid, :] = v_ref[...]
        obuf[0:valid, :] = o_ref[...]
        gbuf[0:valid, :] = if_ref[...]
        v_all, o_all, g_raw = vbuf[...], obuf[...], gbuf[...]
    else:
        xbuf[SUBLANES:SUBLANES + chunk, :] = qk_ref[...]
        v_all, o_all, g_raw = v_ref[...], o_ref[...], if_ref[...]

    conv = cb_ref[...]
    for j in range(CONV_W):
        off = SUBLANES - (CONV_W - 1) + j
        conv = conv + cw_ref[j:j + 1, :] * xbuf[off:off + chunk, :]
    new_hist = xbuf[valid:valid + SUBLANES, :]
    xbuf[0:SUBLANES, :] = new_hist
    act = conv * _sigmoid(conv)
    if valid < chunk:
        rows = lax.broadcasted_iota(jnp.int32, (chunk, 1), 0)
        act = jnp.where(rows < valid, act, 0.0)

    lane = lax.broadcasted_iota(jnp.int32, (chunk, LANES), 1)
    is_f = (lane >= M_HEADS) & (lane < 2 * M_HEADS)
    lf = jnp.minimum(g_raw, 0.0) - jnp.log(1.0 + jnp.exp(-jnp.abs(g_raw)))
    g_lin = jnp.where(is_f, lf, g_raw)
    if valid < chunk:
        rows_l = lax.broadcasted_iota(jnp.int32, (chunk, LANES), 0)
        g_lin = jnp.where((rows_l >= valid) & is_f, 0.0, g_lin)
    ri = lax.broadcasted_iota(jnp.int32, (chunk, chunk), 0)
    ci = lax.broadcasted_iota(jnp.int32, (chunk, chunk), 1)
    causal = ci <= ri
    tri = causal.astype(F32)
    csum = jnp.dot(tri, g_lin, preferred_element_type=F32, precision=lax.Precision.HIGHEST)
    m_col = jnp.where(is_f, csum, g_lin)
    sel = (lax.broadcasted_iota(jnp.int32, (SUBLANES, LANES), 0)
           == lax.broadcasted_iota(jnp.int32, (SUBLANES, LANES), 1)).astype(F32)
    m_row = _nt_dot(sel, m_col, precision=lax.Precision.HIGHEST)
    if valid < chunk:
        tcol = lax.broadcasted_iota(jnp.int32, (chunk, 1), 0)
        trow = lax.broadcasted_iota(jnp.int32, (1, chunk), 1)

    for h in range(M_HEADS):
        q = act[:, h * M_DQK:(h + 1) * M_DQK]
        k = act[:, QK_W // 2 + h * M_DQK:QK_W // 2 + (h + 1) * M_DQK] * (M_DQK ** -0.5)
        v = v_all[:, h * M_DV:(h + 1) * M_DV]
        qb, kb, vb = q.astype(BF16), k.astype(BF16), v.astype(BF16)
        li_col = m_col[:, h:h + 1]
        b_col = m_col[:, M_HEADS + h:M_HEADS + h + 1]
        li_row = m_row[h:h + 1, :]
        b_row = m_row[M_HEADS + h:M_HEADS + h + 1, :]
        if valid < chunk:
            li_col = jnp.where(tcol < valid, li_col, NEG_INF)
            li_row = jnp.where(trow < valid, li_row, NEG_INF)
        m_prev = m_s[h:h + 1, 0:1]
        c_prev = c_s[h]
        n_prev = n_s[h:h + 1, :]

        dmat = jnp.where(causal, b_col - b_row + li_row, NEG_INF)
        inter = b_col + m_prev
        m_t = jnp.maximum(inter, jnp.max(dmat, axis=-1, keepdims=True))
        s_qk = _nt_dot(qb, kb)
        a = jnp.exp(dmat - m_t) * s_qk
        sc = jnp.exp(inter - m_t)
        num = sc * jnp.dot(qb, c_prev.astype(BF16), preferred_element_type=F32) \
            + jnp.dot(a.astype(BF16), vb, preferred_element_type=F32)
        den = sc * jnp.sum(q * n_prev, axis=-1, keepdims=True) + jnp.sum(a, axis=-1, keepdims=True)
        hh = num / jnp.maximum(jnp.abs(den), jnp.exp(-m_t))

        b_end = b_col[chunk - 1:chunk, :]
        g_col = b_end - b_col + li_col
        g_row = b_end - b_row + li_row
        m_new = jnp.maximum(b_end + m_prev, jnp.max(g_row, axis=-1, keepdims=True))
        w_col = jnp.exp(g_col - m_new)
        decay = jnp.exp(b_end + m_prev - m_new)
        c_s[h] = decay * c_prev + _tn_dot(kb, (w_col * v).astype(BF16))
        n_s[h:h + 1, :] = decay * n_prev + jnp.sum(w_col * k, axis=0, keepdims=True)
        m_s[h:h + 1, :] = jnp.broadcast_to(m_new, (1, LANES))

        y = _ln(hh) * nw_ref[:, h * M_DV:(h + 1) * M_DV] * _sigmoid(o_all[:, h * M_DV:(h + 1) * M_DV])
        hm_ref[:, h * M_DV:(h + 1) * M_DV] = y[0:valid, :]

    @pl.when(c_idx == last)
    def _():
        cout_ref[...] = c_s[...]
        nout_ref[...] = n_s[...]
        mout_ref[...] = m_s[...]
        hout_ref[...] = xbuf[0:SUBLANES, :]


def _mlstm(qk, v, o, ifg, conv_w, conv_b, norm_w, hist0, c0, n0, m0, *, n_seq, n_chunks, chunk, valid):
    if valid == chunk:
        tok = lambda width: pl.BlockSpec((chunk, width), lambda b, c: (b * n_chunks + c, 0))
        args = (qk, v, o, ifg)
        hm_shape = jax.ShapeDtypeStruct((n_seq * n_chunks * valid, MV_W), F32)
    else:
        tok = lambda width: pl.BlockSpec((None, valid, width), lambda b, c: (b * n_chunks + c, 0, 0))
        args = tuple(a.reshape(n_seq * n_chunks, valid, a.shape[-1]) for a in (qk, v, o, ifg))
        hm_shape = jax.ShapeDtypeStruct((n_seq * n_chunks, valid, MV_W), F32)
    const2 = lambda r, w: pl.BlockSpec((r, w), lambda b, c: (0, 0))
    per_seq = lambda *dims: pl.BlockSpec((None,) + dims, lambda b, c: (b,) + (0,) * len(dims))
    outs = pl.pallas_call(
        functools.partial(_mlstm_kernel, chunk=chunk, valid=valid),
        out_shape=[hm_shape,
                   jax.ShapeDtypeStruct((n_seq, M_HEADS, M_DQK, M_DV), F32),
                   jax.ShapeDtypeStruct((n_seq, SUBLANES, M_DQK), F32),
                   jax.ShapeDtypeStruct((n_seq, SUBLANES, LANES), F32),
                   jax.ShapeDtypeStruct((n_seq, SUBLANES, QK_W), F32)],
        grid=(n_seq, n_chunks),
        in_specs=[tok(QK_W), tok(MV_W), tok(MV_W), tok(LANES),
                  const2(CONV_W, QK_W), const2(1, QK_W), const2(1, MV_W),
                  per_seq(SUBLANES, QK_W), per_seq(M_HEADS, M_DQK, M_DV), per_seq(SUBLANES, M_DQK),
                  per_seq(SUBLANES, LANES)],
        out_specs=[tok(MV_W), per_seq(M_HEADS, M_DQK, M_DV), per_seq(SUBLANES, M_DQK),
                   per_seq(SUBLANES, LANES), per_seq(SUBLANES, QK_W)],
        scratch_shapes=[pltpu.VMEM((chunk + 2 * SUBLANES, QK_W), F32), pltpu.VMEM((chunk, MV_W), F32),
                        pltpu.VMEM((chunk, MV_W), F32), pltpu.VMEM((chunk, LANES), F32),
                        pltpu.VMEM((M_HEADS, M_DQK, M_DV), F32), pltpu.VMEM((SUBLANES, M_DQK), F32),
                        pltpu.VMEM((SUBLANES, LANES), F32)],
        compiler_params=pltpu.CompilerParams(dimension_semantics=("arbitrary", "arbitrary"),
                                             vmem_limit_bytes=VMEM_LIMIT),
        name="mlstm",
    )(*args, conv_w, conv_b.reshape(1, -1), norm_w.reshape(1, -1), hist0, c0, n0, m0)
    hm = outs[0].reshape(n_seq * n_chunks * valid, MV_W)
    return hm, outs[1], outs[2], outs[3], outs[4]


def _pad_rows(a, rows, at_end):
    pad = rows - a.shape[1]
    cfg = ((0, 0), (pad, 0), (0, 0)) if at_end else ((0, 0), (0, pad), (0, 0))
    return jnp.pad(a, cfg)


def _attn_prompt_kernel(q_ref, kvc_ref, kvp_ref, o_ref, l_ref):
    n = pl.program_id(2)
    qi = lax.broadcasted_iota(jnp.int32, (ATTN_BLK, ATTN_BLK), 0)
    ki = lax.broadcasted_iota(jnp.int32, (ATTN_BLK, ATTN_BLK), 1)
    mask_c = ki <= qi
    mask_p = (ki >= qi) & (n > 0)
    lane_head = lax.broadcasted_iota(jnp.int32, (ATTN_BLK, A_W), 1) // A_DH
    q = q_ref[...] * (A_DH ** -0.5)
    qs = jnp.concatenate([jnp.where(lane_head == h, q, 0.0) for h in range(A_HEADS)], axis=0).astype(BF16)
    kc, vc = kvc_ref[:, 0:A_W].astype(BF16), kvc_ref[:, A_W:2 * A_W].astype(BF16)
    kp, vp = kvp_ref[:, 0:A_W].astype(BF16), kvp_ref[:, A_W:2 * A_W].astype(BF16)
    mask_c = jnp.concatenate([mask_c] * A_HEADS, axis=0)
    mask_p = jnp.concatenate([mask_p] * A_HEADS, axis=0)
    s_c = jnp.where(mask_c, _nt_dot(qs, kc), NEG_INF)
    s_p = jnp.where(mask_p, _nt_dot(qs, kp), NEG_INF)
    mx = jnp.max(jnp.maximum(s_c, s_p), axis=-1, keepdims=True)
    p_c = jnp.exp(s_c - mx)
    p_p = jnp.exp(s_p - mx)
    l = jnp.sum(p_c + p_p, axis=-1, keepdims=True)
    inv = 1.0 / l
    pv = jnp.dot((p_c * inv).astype(BF16), vc, preferred_element_type=F32) \
        + jnp.dot((p_p * inv).astype(BF16), vp, preferred_element_type=F32)
    lse = mx + jnp.log(l)
    o = jnp.zeros((ATTN_BLK, A_W), F32)
    le = jnp.zeros((ATTN_BLK, A_W), F32)
    for h in range(A_HEADS):
        rows = slice(h * ATTN_BLK, (h + 1) * ATTN_BLK)
        o = jnp.where(lane_head == h, pv[rows, :], o)
        le = jnp.where(lane_head == h, lse[rows, :], le)
    o_ref[...] = o
    l_ref[...] = le


def _attn_prompt(q, kv, dilation):
    t_d = SEQ // dilation
    nb = t_d // ATTN_BLK
    q3 = q.reshape(BATCH, t_d, dilation * A_W)
    kv3 = kv.reshape(BATCH, t_d, dilation * 2 * A_W)
    qspec = pl.BlockSpec((None, ATTN_BLK, A_W), lambda b, r, n: (b, n, r))
    o, l = pl.pallas_call(
        _attn_prompt_kernel,
        out_shape=[jax.ShapeDtypeStruct(q3.shape, F32)] * 2,
        grid=(BATCH, dilation, nb),
        in_specs=[qspec,
                  pl.BlockSpec((None, ATTN_BLK, 2 * A_W), lambda b, r, n: (b, n, r)),
                  pl.BlockSpec((None, ATTN_BLK, 2 * A_W), lambda b, r, n: (b, jnp.maximum(n - 1, 0), r))],
        out_specs=[qspec, qspec],
        name="attn_prompt",
    )(q3, kv3, kv3)
    return o.reshape(BATCH * SEQ, A_W), l.reshape(BATCH * SEQ, A_W)


def _attn_sample_group(qpad, kvpad, cache_ref, window, dilation):
    cache_len = cache_ref.shape[1]
    n_rows = A_HEADS * SUBLANES
    tok = lax.broadcasted_iota(jnp.int32, (n_rows, 1), 0) & (SUBLANES - 1)
    delta = cache_len + tok - lax.broadcasted_iota(jnp.int32, (1, cache_len), 1)
    ok_c = ((delta & (dilation - 1)) == 0) & (delta >= dilation) & (delta <= window)
    npos = lax.broadcasted_iota(jnp.int32, (1, SUBLANES), 1)
    dnew = tok - npos
    ok_n = (dnew >= 0) & ((dnew & (dilation - 1)) == 0) & (dnew <= window) & (npos < DEC_SEQ)
    lane_head = lax.broadcasted_iota(jnp.int32, (SUBLANES, A_W), 1) // A_DH
    qs = qpad * (A_DH ** -0.5)
    qs = jnp.concatenate([jnp.where(lane_head == h, qs, 0.0) for h in range(A_HEADS)], axis=0).astype(BF16)
    kt = cache_ref[0:A_W, :].astype(BF16)
    vt = cache_ref[A_W:2 * A_W, :].astype(BF16)
    s_c = jnp.where(ok_c, jnp.dot(qs, kt, preferred_element_type=F32), NEG_INF)
    s_n = jnp.where(ok_n, _nt_dot(qs, kvpad[:, 0:A_W].astype(BF16)), NEG_INF)
    mx = jnp.maximum(jnp.max(s_c, axis=-1, keepdims=True), jnp.max(s_n, axis=-1, keepdims=True))
    p_c = jnp.exp(s_c - mx)
    p_n = jnp.exp(s_n - mx)
    l = jnp.sum(p_c, axis=-1, keepdims=True) + jnp.sum(p_n, axis=-1, keepdims=True)
    inv = 1.0 / l
    pv = _nt_dot((p_c * inv).astype(BF16), vt) \
        + jnp.dot((p_n * inv).astype(BF16), kvpad[:, A_W:2 * A_W].astype(BF16), preferred_element_type=F32)
    lse = mx + jnp.log(l)
    o = jnp.zeros((SUBLANES, A_W), F32)
    le = jnp.zeros((SUBLANES, A_W), F32)
    for h in range(A_HEADS):
        rows = slice(h * SUBLANES, (h + 1) * SUBLANES)
        o = jnp.where(lane_head == h, pv[rows, :], o)
        le = jnp.where(lane_head == h, lse[rows, :], le)
    return o, le


def _attn_sample_kernel(q0_ref, kv0_ref, c0_ref, q1_ref, kv1_ref, c1_ref, q2_ref, kv2_ref, c2_ref,
                        o0_ref, l0_ref, o1_ref, l1_ref, o2_ref, l2_ref, qpad, kvpad):
    ins = ((q0_ref, kv0_ref, c0_ref, o0_ref, l0_ref), (q1_ref, kv1_ref, c1_ref, o1_ref, l1_ref),
           (q2_ref, kv2_ref, c2_ref, o2_ref, l2_ref))
    qpad[...] = jnp.zeros((SUBLANES, A_W), F32)
    kvpad[...] = jnp.zeros((SUBLANES, 2 * A_W), F32)
    for (window, dilation), (q_ref, kv_ref, c_ref, o_ref, l_ref) in zip(ATTN_GROUPS, ins):
        qpad[0:DEC_SEQ, :] = q_ref[...]
        kvpad[0:DEC_SEQ, :] = kv_ref[...]
        o, le = _attn_sample_group(qpad[...], kvpad[...], c_ref, window, dilation)
        o_ref[...] = o[0:DEC_SEQ, :]
        l_ref[...] = le[0:DEC_SEQ, :]


def _attn_sample(qs, kvs, caches):
    args, in_specs = [], []
    tok = lambda width: pl.BlockSpec((None, DEC_SEQ, width), lambda b: (b, 0, 0))
    for q, kv, cache in zip(qs, kvs, caches):
        cache_len = cache.shape[1]
        c3 = jnp.transpose(cache, (0, 2, 3, 4, 1)).reshape(DEC_BATCH, 2 * A_W, cache_len)
        args += [q.reshape(DEC_BATCH, DEC_SEQ, A_W), kv.reshape(DEC_BATCH, DEC_SEQ, 2 * A_W), c3]
        in_specs += [tok(A_W), tok(2 * A_W), pl.BlockSpec((None, 2 * A_W, cache_len), lambda b: (b, 0, 0))]
    outs = pl.pallas_call(
        _attn_sample_kernel,
        out_shape=[jax.ShapeDtypeStruct((DEC_BATCH, DEC_SEQ, A_W), F32)] * (2 * N_GROUPS),
        grid=(DEC_BATCH,),
        in_specs=in_specs,
        out_specs=[tok(A_W)] * (2 * N_GROUPS),
        scratch_shapes=[pltpu.VMEM((SUBLANES, A_W), F32), pltpu.VMEM((SUBLANES, 2 * A_W), F32)],
        compiler_params=pltpu.CompilerParams(vmem_limit_bytes=VMEM_LIMIT),
        name="attn_sample",
    )(*args)
    return [a.reshape(DEC_BATCH * DEC_SEQ, A_W) for a in outs]


def _merge_kernel(hm_ref, o0_ref, l0_ref, o1_ref, l1_ref, o2_ref, l2_ref, g_ref, x_ref, g1_ref, sc2_ref, sh2_ref,
                  wbm_ref, wba_ref, wo_ref, lng_ref, lnb_ref, wr_ref, br_ref, cnt0_ref,
                  x1_ref, h2_ref, ei_ref, eg_ref, er_ref, cnt_ref, run_s):
    i = pl.program_id(0)

    @pl.when(i == 0)
    def _():
        run_s[...] = cnt0_ref[...]

    l0, l1, l2 = l0_ref[...], l1_ref[...], l2_ref[...]
    mx = jnp.maximum(jnp.maximum(l0, l1), l2)
    e0, e1, e2 = jnp.exp(l0 - mx), jnp.exp(l1 - mx), jnp.exp(l2 - mx)
    inv = 1.0 / (e0 + e1 + e2)
    attn = (e0 * inv) * o0_ref[...] + (e1 * inv) * o1_ref[...] + (e2 * inv) * o2_ref[...]
    br_m = jnp.dot(hm_ref[...].astype(BF16), wbm_ref[...], preferred_element_type=F32)
    br_a = jnp.dot(attn.astype(BF16), wba_ref[...], preferred_element_type=F32)
    merged = g_ref[:, 0:D_MODEL] * br_m + g_ref[:, D_MODEL:2 * D_MODEL] * br_a
    mix = jnp.dot(merged.astype(BF16), wo_ref[...], preferred_element_type=F32)
    x1 = _ln(DN_ALPHA * x_ref[...] + g1_ref[...] * mix) * lng_ref[...] + lnb_ref[...]
    x1_ref[...] = x1
    h2 = _ln(x1) * (1.0 + sc2_ref[...]) + sh2_ref[...]
    h2_ref[...] = h2

    lane = lax.broadcasted_iota(jnp.int32, (TOK_TILE, LANES), 1)
    lane_f = lane.astype(F32)
    logits = jnp.dot(h2.astype(BF16), wr_ref[...], preferred_element_type=F32) + br_ref[...]
    cur = jnp.where(lane < N_EXPERTS, logits, NEG_INF)
    vals, onehots = [], []
    idx_mat = jnp.zeros((TOK_TILE, LANES), F32)
    for k in range(TOP_K):
        m = jnp.max(cur, axis=-1, keepdims=True)
        sel = jnp.min(jnp.where(cur == m, lane_f, float(LANES)), axis=-1, keepdims=True)
        hit = lane_f == sel
        vals.append(m)
        onehots.append(hit)
        idx_mat = jnp.where(lane == k, sel, idx_mat)
        cur = jnp.where(hit, NEG_INF, cur)
    es = [jnp.exp(v - vals[0]) for v in vals]
    tot = es[0] + es[1] + es[2] + es[3]
    gate_mat = jnp.zeros((TOK_TILE, LANES), F32)
    for k in range(TOP_K):
        gate_mat = jnp.where(lane == k, es[k] / tot, gate_mat)

    cnt = jnp.zeros((TOK_TILE, LANES), F32)
    for hit in onehots:
        cnt = cnt + hit.astype(F32)
    ri = lax.broadcasted_iota(jnp.int32, (TOK_TILE, TOK_TILE), 0)
    ci = lax.broadcasted_iota(jnp.int32, (TOK_TILE, TOK_TILE), 1)
    before = (ci < ri).astype(BF16)
    base = run_s[0:1, :] + jnp.dot(before, cnt.astype(BF16), preferred_element_type=F32)
    rank_mat = jnp.zeros((TOK_TILE, LANES), F32)
    for k, hit in enumerate(onehots):
        rk = jnp.sum(jnp.where(hit, base, 0.0), axis=-1, keepdims=True)
        rank_mat = jnp.where(lane == k, rk, rank_mat)
    new_run = run_s[0:1, :] + jnp.sum(cnt, axis=0, keepdims=True)
    run_s[...] = jnp.broadcast_to(new_run, (SUBLANES, LANES))
    cnt_ref[...] = run_s[...]
    ei_ref[...] = idx_mat[:, 0:TOP_K].astype(jnp.int32)
    eg_ref[...] = gate_mat[:, 0:TOP_K]
    er_ref[...] = rank_mat[:, 0:TOP_K].astype(jnp.int32)


def _merge(hm, attn_o, attn_l, gates, x2d, g1, sc2, sh2, wts, cnt0, per_token, tiles_per_batch):
    n = x2d.shape[0]
    tok = lambda width: pl.BlockSpec((TOK_TILE, width), lambda i: (i, 0))
    const = lambda a: pl.BlockSpec(a.shape, lambda i: (0,) * a.ndim)
    mod = _mod_spec(per_token, tiles_per_batch)
    attn_args, attn_specs = [], []
    for o, l in zip(attn_o, attn_l):
        attn_args += [o, l]
        attn_specs += [tok(A_W), tok(A_W)]
    return pl.pallas_call(
        _merge_kernel,
        out_shape=[jax.ShapeDtypeStruct((n, D_MODEL), F32), jax.ShapeDtypeStruct((n, D_MODEL), F32),
                   jax.ShapeDtypeStruct((n, TOP_K), jnp.int32), jax.ShapeDtypeStruct((n, TOP_K), F32),
                   jax.ShapeDtypeStruct((n, TOP_K), jnp.int32), jax.ShapeDtypeStruct((SUBLANES, LANES), F32)],
        grid=(n // TOK_TILE,),
        in_specs=[tok(MV_W)] + attn_specs + [tok(2 * D_MODEL), tok(D_MODEL), mod, mod, mod]
                 + [const(w) for w in wts] + [const(cnt0)],
        out_specs=[tok(D_MODEL), tok(D_MODEL), tok(TOP_K), tok(TOP_K), tok(TOP_K),
                   pl.BlockSpec((SUBLANES, LANES), lambda i: (0, 0))],
        scratch_shapes=[pltpu.VMEM((SUBLANES, LANES), F32)],
        compiler_params=pltpu.CompilerParams(dimension_semantics=("arbitrary",), vmem_limit_bytes=VMEM_LIMIT),
        name="merge_router",
    )(hm, *attn_args, gates, x2d, g1, sc2, sh2, *wts, cnt0)


def _row_copy(src_hbm, src_row, dst, dst_row, sem):
    return pltpu.make_async_copy(src_hbm.at[pl.ds(src_row, 1)], dst.at[pl.ds(dst_row, 1)], sem)


def _dispatch_kernel(pend_ref, dest_ref, h2p_ref, h2s_ref, xs_hbm, zeros, sem, *, n_prompt_tiles):
    i = pl.program_id(0)

    @pl.when(i == 0)
    def _():
        zeros[...] = jnp.zeros((MOE_TILE, D_MODEL), F32)
        n_blk = xs_hbm.shape[0] // MOE_TILE
        used_blk = pend_ref[N_EXPERTS - 1] // MOE_TILE

        def zero_block(start):
            return pltpu.make_async_copy(zeros, xs_hbm.at[pl.ds(pl.multiple_of(start, MOE_TILE), MOE_TILE)], sem)

        def last_block(e):
            return zero_block(jnp.maximum(pend_ref[e] - MOE_TILE, 0))

        def fill(e, carry):
            last_block(e).start()
            return carry

        def fill_done(e, carry):
            last_block(e).wait()
            return carry

        def tail(b, carry):
            zero_block(b * MOE_TILE).start()
            return carry

        def tail_done(b, carry):
            zero_block(b * MOE_TILE).wait()
            return carry

        lax.fori_loop(0, N_EXPERTS, fill, 0)
        lax.fori_loop(used_blk, n_blk, tail, 0)
        lax.fori_loop(0, N_EXPERTS, fill_done, 0)
        lax.fori_loop(used_blk, n_blk, tail_done, 0)

    def scatter(h2_ref):
        def issue(t, carry):
            for k in range(TOP_K):
                _row_copy(h2_ref, t, xs_hbm, dest_ref[t * TOP_K + k], sem).start(priority=k % 2)
            return carry

        def drain(j, carry):
            _row_copy(h2_ref, 0, xs_hbm, 0, sem).wait()
            return carry

        lax.fori_loop(0, TOK_TILE, issue, 0)
        lax.fori_loop(0, TOK_TILE * TOP_K, drain, 0, unroll=8)

    @pl.when(i < n_prompt_tiles)
    def _():
        scatter(h2p_ref)

    @pl.when(i >= n_prompt_tiles)
    def _():
        scatter(h2s_ref)


def _dispatch(pend, dest_flat, h2_p, h2_s, n_rows):
    tp, ts = h2_p.shape[0] // TOK_TILE, h2_s.shape[0] // TOK_TILE
    return pl.pallas_call(
        functools.partial(_dispatch_kernel, n_prompt_tiles=tp),
        out_shape=jax.ShapeDtypeStruct((n_rows, D_MODEL), F32),
        grid_spec=pltpu.PrefetchScalarGridSpec(
            num_scalar_prefetch=1, grid=(tp + ts,),
            in_specs=[pl.BlockSpec((TOK_TILE * TOP_K,), lambda i, pe: (i,), memory_space=pltpu.SMEM),
                      pl.BlockSpec((TOK_TILE, D_MODEL), lambda i, pe: (jnp.minimum(i, tp - 1), 0)),
                      pl.BlockSpec((TOK_TILE, D_MODEL), lambda i, pe: (jnp.maximum(i - tp, 0), 0))],
            out_specs=pl.BlockSpec(memory_space=pl.ANY),
            scratch_shapes=[pltpu.VMEM((MOE_TILE, D_MODEL), F32), pltpu.SemaphoreType.DMA(())]),
        compiler_params=pltpu.CompilerParams(dimension_semantics=("arbitrary",)),
        name="moe_dispatch",
    )(pend, dest_flat, h2_p, h2_s)


CAST_ROWS = 128


def _expert_kernel(be_ref, bv_ref, xs_ref, wu_ref, bu_ref, wd_ref, bd_ref, ys_ref, wu_bf, wd_bf):
    i = pl.program_id(0)

    @pl.when((i == 0) | (be_ref[i] != be_ref[jnp.maximum(i - 1, 0)]))
    def _():
        def cast(r, carry):
            rows = pl.ds(pl.multiple_of(r * CAST_ROWS, CAST_ROWS), CAST_ROWS)
            wu_bf[rows, :] = wu_ref[rows, :].astype(BF16)
            wd_bf[rows, :] = wd_ref[rows, :].astype(BF16)
            return carry

        lax.fori_loop(0, D_MODEL // CAST_ROWS, cast, 0)

    @pl.when(bv_ref[i] == 1)
    def _():
        hu = jnp.dot(xs_ref[...].astype(BF16), wu_bf[...], preferred_element_type=F32) + bu_ref[...]
        x_glu = jnp.minimum(hu[:, 0:D_FF], SWIGLU_LIMIT)
        x_lin = jnp.clip(hu[:, D_FF:2 * D_FF], -SWIGLU_LIMIT, SWIGLU_LIMIT)
        act = x_glu * _sigmoid(SWIGLU_ALPHA * x_glu) * (x_lin + 1.0)
        ys_ref[...] = jnp.dot(act.astype(BF16), wd_bf[...], preferred_element_type=F32) + bd_ref[...]

    @pl.when(bv_ref[i] == 0)
    def _():
        ys_ref[...] = jnp.zeros((MOE_TILE, D_MODEL), F32)


def _experts(blk_e, blk_valid, xs, w_up, b_up, w_down, b_down):
    assert D_FF == D_MODEL
    n_blk = blk_e.shape[0]
    return pl.pallas_call(
        _expert_kernel,
        out_shape=jax.ShapeDtypeStruct(xs.shape, F32),
        grid_spec=pltpu.PrefetchScalarGridSpec(
            num_scalar_prefetch=2,
            grid=(n_blk,),
            in_specs=[pl.BlockSpec((MOE_TILE, D_MODEL), lambda i, be, bv: (i * bv[i], 0)),
                      pl.BlockSpec((None, D_MODEL, 2 * D_FF), lambda i, be, bv: (be[i], 0, 0)),
                      pl.BlockSpec((None, 1, 2 * D_FF), lambda i, be, bv: (be[i], 0, 0)),
                      pl.BlockSpec((None, D_FF, D_MODEL), lambda i, be, bv: (be[i], 0, 0)),
                      pl.BlockSpec((None, 1, D_MODEL), lambda i, be, bv: (be[i], 0, 0))],
            out_specs=pl.BlockSpec((MOE_TILE, D_MODEL), lambda i, be, bv: (i, 0)),
            scratch_shapes=[pltpu.VMEM((D_MODEL, 2 * D_FF), BF16), pltpu.VMEM((D_FF, D_MODEL), BF16)]),
        compiler_params=pltpu.CompilerParams(dimension_semantics=("arbitrary",), vmem_limit_bytes=VMEM_LIMIT),
        name="moe_experts",
    )(blk_e, blk_valid, xs, w_up, b_up.reshape(N_EXPERTS, 1, -1), w_down, b_down.reshape(N_EXPERTS, 1, -1))


def _combine_kernel(dest_ref, ys_hbm, gate_ref, x1_ref, g2_ref, lng_ref, lnb_ref, out_ref, rows, sem):
    def issue(t, carry):
        for k in range(TOP_K):
            _row_copy(ys_hbm, dest_ref[t * TOP_K + k], rows.at[k], t, sem).start(priority=k % 2)
        return carry

    def drain(j, carry):
        _row_copy(ys_hbm, 0, rows.at[0], 0, sem).wait()
        return carry

    lax.fori_loop(0, TOK_TILE, issue, 0)
    lax.fori_loop(0, TOK_TILE * TOP_K, drain, 0, unroll=8)
    gate = gate_ref[...]
    f = gate[:, 0:1] * rows[0]
    for k in range(1, TOP_K):
        f = f + gate[:, k:k + 1] * rows[k]
    out_ref[...] = _ln(DN_ALPHA * x1_ref[...] + g2_ref[...] * f) * lng_ref[...] + lnb_ref[...]


def _combine(dest_flat, ys, gate, x1, g2, ln_g, ln_b, per_token, tiles_per_batch):
    n = x1.shape[0]
    tok = lambda width: pl.BlockSpec((TOK_TILE, width), lambda i: (i, 0))
    const = pl.BlockSpec((1, D_MODEL), lambda i: (0, 0))
    return pl.pallas_call(
        _combine_kernel,
        out_shape=jax.ShapeDtypeStruct((n, D_MODEL), F32),
        grid=(n // TOK_TILE,),
        in_specs=[pl.BlockSpec((TOK_TILE * TOP_K,), lambda i: (i,), memory_space=pltpu.SMEM),
                  pl.BlockSpec(memory_space=pl.ANY), tok(TOP_K), tok(D_MODEL),
                  _mod_spec(per_token, tiles_per_batch), const, const],
        out_specs=tok(D_MODEL),
        scratch_shapes=[pltpu.VMEM((TOP_K, TOK_TILE, D_MODEL), F32), pltpu.SemaphoreType.DMA(())],
        compiler_params=pltpu.CompilerParams(dimension_semantics=("arbitrary",), vmem_limit_bytes=VMEM_LIMIT),
        name="moe_combine",
    )(dest_flat, ys, gate, x1, g2, ln_g.reshape(1, -1), ln_b.reshape(1, -1))


def _routing(e_idx, rank, counts):
    n_assign = e_idx.shape[0] * TOP_K
    n_blk = n_assign // MOE_TILE + N_EXPERTS
    padded = (counts + MOE_TILE - 1) // MOE_TILE * MOE_TILE
    pend = jnp.cumsum(padded)
    pstart = pend - padded
    dest = pstart[e_idx] + rank
    blk_row = jnp.arange(n_blk, dtype=jnp.int32) * MOE_TILE
    blk_e = jnp.minimum(jnp.sum((pend[None, :] <= blk_row[:, None]).astype(jnp.int32), axis=1), N_EXPERTS - 1)
    blk_valid = (blk_row < pend[-1]).astype(jnp.int32)
    return dest.reshape(-1).astype(jnp.int32), pend.astype(jnp.int32), blk_e, blk_valid, n_blk * MOE_TILE


def kernel(x_prompt, x_sample, c_prompt, c_sample, cache_kv_w128, cache_kv_w512, cache_kv_w2048,
           state_mlstm_C, state_mlstm_n, state_mlstm_m, state_qk_conv,
           w_ada, b_ada, w_in, b_in, conv_w, conv_b, mh_norm_w, w_br_mlstm, w_br_attn,
           w_merge, b_merge, w_o, ln1_g, ln1_b, w_router, b_router, w_up, b_up, w_down, b_down,
           ln2_g, ln2_b):
    assert DEPTH == 1 and SEQ % (ATTN_GROUPS[-1][0]) == 0 and SEQ % MLSTM_CHUNK == 0
    n_p, n_s = BATCH * SEQ, DEC_BATCH * DEC_SEQ
    tiles_per_batch = SEQ // TOK_TILE
    layer = 0

    wi, bi = w_in[layer], b_in[layer]
    pad_if = LANES - 2 * M_HEADS
    w1 = jnp.concatenate([wi[:, :OFF_V], wi[:, OFF_V:OFF_I], wi[:, OFF_O:OFF_A], wi[:, OFF_A:], w_merge[layer],
                          wi[:, OFF_I:OFF_O], jnp.zeros((D_MODEL, pad_if), F32)], axis=1).astype(BF16)
    b1 = jnp.concatenate([bi[:OFF_V], bi[OFF_V:OFF_I], bi[OFF_O:OFF_A], bi[OFF_A:], b_merge[layer],
                          bi[OFF_I:OFF_O], jnp.zeros((pad_if,), F32)]).reshape(1, -1)
    wr = jnp.pad(w_router[layer], ((0, 0), (0, LANES - N_EXPERTS))).astype(BF16)
    br = jnp.pad(b_router[layer], (0, LANES - N_EXPERTS)).reshape(1, -1)
    merge_w = (w_br_mlstm[layer].astype(BF16), w_br_attn[layer].astype(BF16), w_o[layer].astype(BF16),
               ln1_g[layer].reshape(1, -1), ln1_b[layer].reshape(1, -1), wr, br)

    mod = _ada(jnp.concatenate([c_prompt, c_sample], axis=0), w_ada[layer], b_ada[layer])
    mod_p = mod[:BATCH].reshape(BATCH, 6, 1, D_MODEL)
    mod_s = jnp.repeat(mod[BATCH:], DEC_SEQ, axis=0).reshape(n_s, 6, D_MODEL)
    mods_p = [mod_p[:, j] for j in range(6)]
    mods_s = [mod_s[:, j] for j in range(6)]

    results = {}
    cnt = jnp.zeros((SUBLANES, LANES), F32)
    stage = []
    for name, x, mods, per_token in (("p", x_prompt, mods_p, False), ("s", x_sample, mods_s, True)):
        x2d = x.reshape(-1, D_MODEL)
        qk, v, o, q0, kv0, q1, kv1, q2, kv2, gates, ifg = _inproj(x2d, mods[1], mods[0], w1, b1, per_token,
                                                                    tiles_per_batch)
        qs, kvs = (q0, q1, q2), (kv0, kv1, kv2)
        if name == "p":
            zeros = lambda *s: jnp.zeros(s, F32)
            hm, c_new, n_new, m_new, hist = _mlstm(
                qk, v, o, ifg, conv_w[layer], conv_b[layer], mh_norm_w[layer],
                zeros(BATCH, SUBLANES, QK_W), zeros(BATCH, M_HEADS, M_DQK, M_DV), zeros(BATCH, SUBLANES, M_DQK),
                zeros(BATCH, SUBLANES, LANES), n_seq=BATCH, n_chunks=SEQ // MLSTM_CHUNK, chunk=MLSTM_CHUNK,
                valid=MLSTM_CHUNK)
            attn = [_attn_prompt(q, kv, dil) for q, kv, (_, dil) in zip(qs, kvs, ATTN_GROUPS)]
            attn_o, attn_l = [a[0] for a in attn], [a[1] for a in attn]
            new_kv = [kv.reshape(BATCH, SEQ, 2 * A_W)[:, SEQ - min(w, SEQ):].reshape(BATCH, -1, 2, A_HEADS, A_DH)
                      for kv, (w, _) in zip(kvs, ATTN_GROUPS)]
            nb = BATCH
        else:
            m0 = jnp.broadcast_to(state_mlstm_m[layer][:, :, None], (DEC_BATCH, M_HEADS, LANES))
            hm, c_new, n_new, m_new, hist = _mlstm(
                qk, v, o, ifg, conv_w[layer], conv_b[layer], mh_norm_w[layer],
                _pad_rows(state_qk_conv[layer], SUBLANES, True), state_mlstm_C[layer],
                _pad_rows(state_mlstm_n[layer], SUBLANES, False), _pad_rows(m0, SUBLANES, False),
                n_seq=DEC_BATCH, n_chunks=1, chunk=SUBLANES, valid=DEC_SEQ)
            caches = (cache_kv_w128[layer], cache_kv_w512[layer], cache_kv_w2048[layer])
            outs = _attn_sample(qs, kvs, caches)
            attn_o, attn_l = outs[0::2], outs[1::2]
            new_kv = [kv.reshape(DEC_BATCH, DEC_SEQ, 2, A_HEADS, A_DH) for kv in kvs]
            nb = DEC_BATCH
        x1, h2, e_idx, e_gate, e_rank, cnt_new = _merge(hm, attn_o, attn_l, gates, x2d, mods[2], mods[4], mods[3],
                                                         merge_w, cnt, per_token, tiles_per_batch)
        results[name] = dict(
            x1=x1, h2=h2, e_idx=e_idx, e_gate=e_gate, e_rank=e_rank, g2=mods[5], per_token=per_token,
            states=(new_kv[0][None], new_kv[1][None], new_kv[2][None], c_new[None],
                    n_new[:, :M_HEADS][None], m_new[:, :M_HEADS, 0][None],
                    hist[:, SUBLANES - (CONV_W - 1):][None]), shape=x.shape)
        cnt = cnt_new
        stage.append(name)

    e_idx = jnp.concatenate([results["p"]["e_idx"], results["s"]["e_idx"]], axis=0)
    e_rank = jnp.concatenate([results["p"]["e_rank"], results["s"]["e_rank"]], axis=0)
    counts = cnt[0, :N_EXPERTS].astype(jnp.int32)
    dest, pend, blk_e, blk_valid, n_rows = _routing(e_idx, e_rank, counts)
    xs = _dispatch(pend, dest, results["p"]["h2"], results["s"]["h2"], n_rows)
    ys = _experts(blk_e, blk_valid, xs, w_up[layer], b_up[layer], w_down[layer], b_down[layer])
    ys_out = {}
    off = 0
    for name in ("p", "s"):
        r = results[name]
        n = r["x1"].shape[0]
        ys_out[name] = _combine(dest[off * TOP_K:(off + n) * TOP_K], ys, r["e_gate"], r["x1"],
                                r["g2"], ln2_g[layer], ln2_b[layer], r["per_token"], tiles_per_batch
                                ).reshape(r["shape"])
        off += n
    return (ys_out["p"], ys_out["s"]) + results["p"]["states"] + results["s"]["states"]
```

```python
import functools

import jax
import jax.numpy as jnp
from jax import lax
from jax.experimental import pallas as pl
from jax.experimental.pallas import tpu as pltpu

D_MODEL = 1024
BATCH = 2
SEQ = 8192
DEPTH = 1
DEC_BATCH = 128
DEC_SEQ = 4
PAST_LEN = 2048

M_HEADS = 4
M_DQK = D_MODEL // 8
M_DV = D_MODEL // 4
CONV_W = 4
ATTN_GROUPS = ((128, 1), (512, 4), (2048, 16))
A_HEADS = 4
A_DH = D_MODEL // 16
N_EXPERTS = 32
TOP_K = 4
D_FF = D_MODEL
SWIGLU_LIMIT = 7.0
SWIGLU_ALPHA = 1.702
DN_ALPHA = (2 * DEPTH) ** 0.25
LN_EPS = 1e-5

QK_W = 2 * M_HEADS * M_DQK
MV_W = M_HEADS * M_DV
A_W = A_HEADS * A_DH
OFF_V = QK_W
OFF_I = OFF_V + MV_W
OFF_F = OFF_I + M_HEADS
OFF_O = OFF_F + M_HEADS
OFF_A = OFF_O + MV_W
N_GROUPS = len(ATTN_GROUPS)
P_W = OFF_A + 3 * A_W * N_GROUPS

LANES = 128
SUBLANES = 8
ATTN_BLK = 128
MLSTM_CHUNK = 128
MLSTM_SAMPLE_PAR = 4
TOK_TILE = 256
MOE_TILE = 512
VMEM_LIMIT = 56 * 1024 * 1024

SEG_QK = 0
SEG_V = SEG_QK + QK_W
SEG_O = SEG_V + MV_W
SEG_A = SEG_O + MV_W
SEG_G = SEG_A + 3 * A_W * N_GROUPS
SEG_IF = SEG_G + 2 * D_MODEL
W1_COLS = SEG_IF + LANES

F32 = jnp.float32
BF16 = jnp.bfloat16
NEG_INF = float("-inf")


def _ln(x):
    mu = jnp.mean(x, axis=-1, keepdims=True)
    xc = x - mu
    var = jnp.mean(xc * xc, axis=-1, keepdims=True)
    return xc * lax.rsqrt(var + LN_EPS)


def _sigmoid(x):
    return 1.0 / (1.0 + jnp.exp(-x))


def _nt_dot(a, b, precision=None):
    return lax.dot_general(a, b, (((1,), (1,)), ((), ())), preferred_element_type=F32, precision=precision)


def _tn_dot(a, b):
    return lax.dot_general(a, b, (((0,), (0,)), ((), ())), preferred_element_type=F32)


def _ada_kernel(c_ref, w_ref, b_ref, o_ref):
    c = c_ref[...]
    s = c * _sigmoid(c)
    o_ref[...] = jnp.dot(s.astype(BF16), w_ref[...].astype(BF16), preferred_element_type=F32) + b_ref[...]


def _ada(c_all, w_ada, b_ada):
    r = c_all.shape[0]
    n_col = 6 * D_MODEL // D_MODEL
    return pl.pallas_call(
        _ada_kernel,
        out_shape=jax.ShapeDtypeStruct((r, 6 * D_MODEL), F32),
        grid=(n_col,),
        in_specs=[pl.BlockSpec((r, D_MODEL), lambda j: (0, 0)),
                  pl.BlockSpec((D_MODEL, D_MODEL), lambda j: (0, j)),
                  pl.BlockSpec((1, D_MODEL), lambda j: (0, j))],
        out_specs=pl.BlockSpec((r, D_MODEL), lambda j: (0, j)),
        name="ada_mod",
    )(c_all, w_ada, b_ada.reshape(1, -1))


def _store_dilated(out_ref, val, scr, dil):
    w = val.shape[1]
    if dil == 1:
        out_ref[...] = val
        return
    for c in range(w // LANES):
        scr[c] = val[:, c * LANES:(c + 1) * LANES]
    for r in range(dil):
        for c in range(w // LANES):
            out_ref[:, r * w + c * LANES:r * w + (c + 1) * LANES] = scr[c, pl.ds(r, TOK_TILE // dil, stride=dil), :]


def _load_dilated(in_ref, scr, dil):
    if dil == 1:
        return in_ref[...]
    w = in_ref.shape[1] // dil
    for r in range(dil):
        for c in range(w // LANES):
            scr[c, pl.ds(r, TOK_TILE // dil, stride=dil), :] = in_ref[:, r * w + c * LANES:r * w + (c + 1) * LANES]
    return jnp.concatenate([scr[c] for c in range(w // LANES)], axis=-1)


def _inproj_kernel(x_ref, sc_ref, sh_ref, w_ref, b_ref,
                   qk_ref, v_ref, o_ref, q0_ref, kv0_ref, q1_ref, kv1_ref, q2_ref, kv2_ref, g_ref, if_ref, scr,
                   *, dils):
    h = _ln(x_ref[...]) * (1.0 + sc_ref[...]) + sh_ref[...]
    hb = h.astype(BF16)

    def seg(a, width):
        return jnp.dot(hb, w_ref[:, a:a + width], preferred_element_type=F32) + b_ref[:, a:a + width]

    qk_ref[...] = seg(SEG_QK, QK_W)
    v_ref[...] = seg(SEG_V, MV_W)
    o_ref[...] = seg(SEG_O, MV_W)
    for gi, (q_ref, kv_ref) in enumerate(((q0_ref, kv0_ref), (q1_ref, kv1_ref), (q2_ref, kv2_ref))):
        base = SEG_A + 3 * A_W * gi
        _store_dilated(q_ref, seg(base, A_W), scr, dils[gi])
        _store_dilated(kv_ref, seg(base + A_W, 2 * A_W), scr, dils[gi])
    g_ref[...] = _sigmoid(seg(SEG_G, 2 * D_MODEL))
    if_ref[...] = seg(SEG_IF, LANES)


def _mod_spec(per_token, tiles_per_batch):
    if per_token:
        return pl.BlockSpec((TOK_TILE, D_MODEL), lambda i: (i, 0))
    return pl.BlockSpec((None, 1, D_MODEL), lambda i: (i // tiles_per_batch, 0, 0))


def _inproj(x2d, sc, sh, w1, b1, per_token, tiles_per_batch, dils):
    n = x2d.shape[0]
    tok = lambda width, d=1: pl.BlockSpec((TOK_TILE // d, d * width), lambda i: (i, 0))
    shape = lambda width, d=1: jax.ShapeDtypeStruct((n // d, d * width), F32)
    layout = [(QK_W, 1), (MV_W, 1), (MV_W, 1)]
    for d in dils:
        layout += [(A_W, d), (2 * A_W, d)]
    layout += [(2 * D_MODEL, 1), (LANES, 1)]
    return pl.pallas_call(
        functools.partial(_inproj_kernel, dils=dils),
        out_shape=[shape(w, d) for w, d in layout],
        grid=(n // TOK_TILE,),
        in_specs=[tok(D_MODEL), _mod_spec(per_token, tiles_per_batch), _mod_spec(per_token, tiles_per_batch),
                  pl.BlockSpec((D_MODEL, W1_COLS), lambda i: (0, 0), pipeline_mode=pl.Buffered(1)),
                  pl.BlockSpec((1, W1_COLS), lambda i: (0, 0))],
        out_specs=[tok(w, d) for w, d in layout],
        scratch_shapes=[pltpu.VMEM((2 * A_W // LANES, TOK_TILE, LANES), F32)],
        compiler_params=pltpu.CompilerParams(vmem_limit_bytes=VMEM_LIMIT),
        name="inproj",
    )(x2d, sc, sh, w1, b1)


def _mlstm_kernel(qk_ref, v_ref, o_ref, if_ref, cw_ref, cb_ref, nw_ref, *rest, chunk, valid, n_par):
    for s in range(n_par):
        _mlstm_seq(qk_ref.at[s], v_ref.at[s], o_ref.at[s], if_ref.at[s], cw_ref, cb_ref, nw_ref,
                   *[r.at[s] for r in rest], chunk=chunk, valid=valid)


def _mlstm_seq(qk_ref, v_ref, o_ref, if_ref, cw_ref, cb_ref, nw_ref, hist0_ref, c0_ref, n0_ref, m0_ref,
               hm_ref, cout_ref, nout_ref, mout_ref, hout_ref,
               xbuf, vbuf, obuf, gbuf, c_s, n_s, m_s, *, chunk, valid):
    c_idx = pl.program_id(1)
    last = pl.num_programs(1) - 1

    @pl.when(c_idx == 0)
    def _():
        xbuf[0:SUBLANES, :] = hist0_ref[...]
        c_s[...] = c0_ref[...]
        n_s[...] = n0_ref[...]
        m_s[...] = m0_ref[...]

    if valid < chunk:
        xbuf[SUBLANES:SUBLANES + chunk, :] = jnp.zeros((chunk, QK_W), F32)
        vbuf[...] = jnp.zeros((chunk, MV_W), F32)
        obuf[...] = jnp.zeros((chunk, MV_W), F32)
        gbuf[...] = jnp.zeros((chunk, LANES), F32)
        xbuf[SUBLANES:SUBLANES + valid, :] = qk_ref[...]
        vbuf[0:valid, :] = v_ref[...]
        obuf[0:valid, :] = o_ref[...]
        gbuf[0:valid, :] = if_ref[...]
        v_all, o_all, g_raw = vbuf[...], obuf[...], gbuf[...]
    else:
        xbuf[SUBLANES:SUBLANES + chunk, :] = qk_ref[...]
        v_all, o_all, g_raw = v_ref[...], o_ref[...], if_ref[...]

    conv = cb_ref[...]
    for j in range(CONV_W):
        off = SUBLANES - (CONV_W - 1) + j
        conv = conv + cw_ref[j:j + 1, :] * xbuf[off:off + chunk, :]
    new_hist = xbuf[valid:valid + SUBLANES, :]
    xbuf[0:SUBLANES, :] = new_hist
    act = conv * _sigmoid(conv)
    if valid < chunk:
        rows = lax.broadcasted_iota(jnp.int32, (chunk, 1), 0)
        act = jnp.where(rows < valid, act, 0.0)

    lane = lax.broadcasted_iota(jnp.int32, (chunk, LANES), 1)
    is_f = (lane >= M_HEADS) & (lane < 2 * M_HEADS)
    lf = jnp.minimum(g_raw, 0.0) - jnp.log(1.0 + jnp.exp(-jnp.abs(g_raw)))
    g_lin = jnp.where(is_f, lf, g_raw)
    if valid < chunk:
        rows_l = lax.broadcasted_iota(jnp.int32, (chunk, LANES), 0)
        g_lin = jnp.where((rows_l >= valid) & is_f, 0.0, g_lin)
    ri = lax.broadcasted_iota(jnp.int32, (chunk, chunk), 0)
    ci = lax.broadcasted_iota(jnp.int32, (chunk, chunk), 1)
    causal = ci <= ri
    tri = causal.astype(F32)
    csum = jnp.dot(tri, g_lin, preferred_element_type=F32, precision=lax.Precision.HIGHEST)
    m_col = jnp.where(is_f, csum, g_lin)
    sel = (lax.broadcasted_iota(jnp.int32, (SUBLANES, LANES), 0)
           == lax.broadcasted_iota(jnp.int32, (SUBLANES, LANES), 1)).astype(F32)
    m_row = _nt_dot(sel, m_col, precision=lax.Precision.HIGHEST)
    if valid < chunk:
        tcol = lax.broadcasted_iota(jnp.int32, (chunk, 1), 0)
        trow = lax.broadcasted_iota(jnp.int32, (1, chunk), 1)

    for h in range(M_HEADS):
        q = act[:, h * M_DQK:(h + 1) * M_DQK]
        k = act[:, QK_W // 2 + h * M_DQK:QK_W // 2 + (h + 1) * M_DQK] * (M_DQK ** -0.5)
        v = v_all[:, h * M_DV:(h + 1) * M_DV]
        qb, kb, vb = q.astype(BF16), k.astype(BF16), v.astype(BF16)
        li_col = m_col[:, h:h + 1]
        b_col = m_col[:, M_HEADS + h:M_HEADS + h + 1]
        li_row = m_row[h:h + 1, :]
        b_row = m_row[M_HEADS + h:M_HEADS + h + 1, :]
        if valid < chunk:
            li_col = jnp.where(tcol < valid, li_col, NEG_INF)
            li_row = jnp.where(trow < valid, li_row, NEG_INF)
        m_prev = m_s[h:h + 1, 0:1]
        c_prev = c_s[h]
        n_prev = n_s[h:h + 1, :]

        dmat = jnp.where(causal, b_col - b_row + li_row, NEG_INF)
        inter = b_col + m_prev
        m_t = jnp.maximum(inter, jnp.max(dmat, axis=-1, keepdims=True))
        s_qk = _nt_dot(qb, kb)
        a = jnp.exp(dmat - m_t) * s_qk
        sc = jnp.exp(inter - m_t)
        num = sc * jnp.dot(qb, c_prev.astype(BF16), preferred_element_type=F32) \
            + jnp.dot(a.astype(BF16), vb, preferred_element_type=F32)
        den = sc * jnp.sum(q * n_prev, axis=-1, keepdims=True) + jnp.sum(a, axis=-1, keepdims=True)
        hh = num / jnp.maximum(jnp.abs(den), jnp.exp(-m_t))

        b_end = b_col[chunk - 1:chunk, :]
        g_col = b_end - b_col + li_col
        g_row = b_end - b_row + li_row
        m_new = jnp.maximum(b_end + m_prev, jnp.max(g_row, axis=-1, keepdims=True))
        w_col = jnp.exp(g_col - m_new)
        decay = jnp.exp(b_end + m_prev - m_new)
        c_s[h] = decay * c_prev + _tn_dot(kb, (w_col * v).astype(BF16))
        n_s[h:h + 1, :] = decay * n_prev + jnp.sum(w_col * k, axis=0, keepdims=True)
        m_s[h:h + 1, :] = jnp.broadcast_to(m_new, (1, LANES))

        y = _ln(hh) * nw_ref[:, h * M_DV:(h + 1) * M_DV] * _sigmoid(o_all[:, h * M_DV:(h + 1) * M_DV])
        hm_ref[:, h * M_DV:(h + 1) * M_DV] = y[0:valid, :]

    @pl.when(c_idx == last)
    def _():
        cout_ref[...] = c_s[...]
        nout_ref[...] = n_s[...]
        mout_ref[...] = m_s[...]
        hout_ref[...] = xbuf[0:SUBLANES, :]


def _mlstm(qk, v, o, ifg, conv_w, conv_b, norm_w, hist0, c0, n0, m0, *, n_seq, n_chunks, chunk, valid, n_par):
    seq_len = n_chunks * valid
    tok = lambda width: pl.BlockSpec((n_par, valid, width), lambda b, c: (b, c, 0))
    args = tuple(a.reshape(n_seq, seq_len, a.shape[-1]) for a in (qk, v, o, ifg))
    const2 = lambda r, w: pl.BlockSpec((r, w), lambda b, c: (0, 0))
    per_seq = lambda *dims: pl.BlockSpec((n_par,) + dims, lambda b, c: (b,) + (0,) * len(dims))
    par = lambda *dims: pltpu.VMEM((n_par,) + dims, F32)
    outs = pl.pallas_call(
        functools.partial(_mlstm_kernel, chunk=chunk, valid=valid, n_par=n_par),
        out_shape=[jax.ShapeDtypeStruct((n_seq, seq_len, MV_W), F32),
                   jax.ShapeDtypeStruct((n_seq, M_HEADS, M_DQK, M_DV), F32),
                   jax.ShapeDtypeStruct((n_seq, SUBLANES, M_DQK), F32),
                   jax.ShapeDtypeStruct((n_seq, SUBLANES, LANES), F32),
                   jax.ShapeDtypeStruct((n_seq, SUBLANES, QK_W), F32)],
        grid=(n_seq // n_par, n_chunks),
        in_specs=[tok(QK_W), tok(MV_W), tok(MV_W), tok(LANES),
                  const2(CONV_W, QK_W), const2(1, QK_W), const2(1, MV_W),
                  per_seq(SUBLANES, QK_W), per_seq(M_HEADS, M_DQK, M_DV), per_seq(SUBLANES, M_DQK),
                  per_seq(SUBLANES, LANES)],
        out_specs=[tok(MV_W), per_seq(M_HEADS, M_DQK, M_DV), per_seq(SUBLANES, M_DQK),
                   per_seq(SUBLANES, LANES), per_seq(SUBLANES, QK_W)],
        scratch_shapes=[par(chunk + 2 * SUBLANES, QK_W), par(chunk, MV_W), par(chunk, MV_W), par(chunk, LANES),
                        par(M_HEADS, M_DQK, M_DV), par(SUBLANES, M_DQK), par(SUBLANES, LANES)],
        compiler_params=pltpu.CompilerParams(dimension_semantics=("arbitrary", "arbitrary"),
                                             vmem_limit_bytes=VMEM_LIMIT),
        name="mlstm",
    )(*args, conv_w, conv_b.reshape(1, -1), norm_w.reshape(1, -1), hist0, c0, n0, m0)
    hm = outs[0].reshape(n_seq * seq_len, MV_W)
    return hm, outs[1], outs[2], outs[3], outs[4]


def _pad_rows(a, rows, at_end):
    pad = rows - a.shape[1]
    cfg = ((0, 0), (pad, 0), (0, 0)) if at_end else ((0, 0), (0, pad), (0, 0))
    return jnp.pad(a, cfg)


ATTN_QBLKS = 2


def _attn_block(q, kc, vc, kp, vp, prev_ok):
    qi = lax.broadcasted_iota(jnp.int32, (ATTN_BLK, ATTN_BLK), 0)
    ki = lax.broadcasted_iota(jnp.int32, (ATTN_BLK, ATTN_BLK), 1)
    mask_c = ki <= qi
    mask_p = (ki >= qi) & prev_ok
    lane_head = lax.broadcasted_iota(jnp.int32, (ATTN_BLK, A_W), 1) // A_DH
    q = q * (A_DH ** -0.5)
    qs = jnp.concatenate([jnp.where(lane_head == h, q, 0.0) for h in range(A_HEADS)], axis=0).astype(BF16)
    mask_c = jnp.concatenate([mask_c] * A_HEADS, axis=0)
    mask_p = jnp.concatenate([mask_p] * A_HEADS, axis=0)
    s_c = jnp.where(mask_c, _nt_dot(qs, kc), NEG_INF)
    s_p = jnp.where(mask_p, _nt_dot(qs, kp), NEG_INF)
    mx = jnp.max(jnp.maximum(s_c, s_p), axis=-1, keepdims=True)
    p_c = jnp.exp(s_c - mx)
    p_p = jnp.exp(s_p - mx)
    l = jnp.sum(p_c + p_p, axis=-1, keepdims=True)
    inv = 1.0 / l
    pv = jnp.dot((p_c * inv).astype(BF16), vc, preferred_element_type=F32) \
        + jnp.dot((p_p * inv).astype(BF16), vp, preferred_element_type=F32)
    lse = mx + jnp.log(l)
    o = jnp.zeros((ATTN_BLK, A_W), F32)
    le = jnp.zeros((ATTN_BLK, A_W), F32)
    for h in range(A_HEADS):
        rows = slice(h * ATTN_BLK, (h + 1) * ATTN_BLK)
        o = jnp.where(lane_head == h, pv[rows, :], o)
        le = jnp.where(lane_head == h, lse[rows, :], le)
    return o, le


def _attn_prompt_kernel(q_ref, kvc_ref, kvp_ref, o_ref, l_ref):
    n = pl.program_id(2)
    for j in range(ATTN_QBLKS):
        rows = slice(j * ATTN_BLK, (j + 1) * ATTN_BLK)
        if j == 0:
            prev_ref, prows, prev_ok = kvp_ref, slice(0, ATTN_BLK), n > 0
        else:
            prev_ref, prows, prev_ok = kvc_ref, slice((j - 1) * ATTN_BLK, j * ATTN_BLK), True
        o, le = _attn_block(q_ref[rows, :],
                            kvc_ref[rows, 0:A_W].astype(BF16), kvc_ref[rows, A_W:2 * A_W].astype(BF16),
                            prev_ref[prows, 0:A_W].astype(BF16), prev_ref[prows, A_W:2 * A_W].astype(BF16), prev_ok)
        o_ref[rows, :] = o
        l_ref[rows, :] = le


def _attn_prompt(q3, kv3, dilation):
    rows = ATTN_QBLKS * ATTN_BLK
    nb = SEQ // dilation // rows
    qspec = pl.BlockSpec((None, rows, A_W), lambda b, r, n: (b, n, r))
    return pl.pallas_call(
        _attn_prompt_kernel,
        out_shape=[jax.ShapeDtypeStruct(q3.shape, F32)] * 2,
        grid=(BATCH, dilation, nb),
        in_specs=[qspec,
                  pl.BlockSpec((None, rows, 2 * A_W), lambda b, r, n: (b, n, r)),
                  pl.BlockSpec((None, ATTN_BLK, 2 * A_W),
                               lambda b, r, n: (b, jnp.maximum(ATTN_QBLKS * n - 1, 0), r))],
        out_specs=[qspec, qspec],
        name="attn_prompt",
    )(q3, kv3, kv3)


def _attn_sample_group(qpad, kvpad, cache_ref, window, dilation):
    cache_len = cache_ref.shape[1]
    n_rows = A_HEADS * SUBLANES
    tok = lax.broadcasted_iota(jnp.int32, (n_rows, 1), 0) & (SUBLANES - 1)
    delta = cache_len + tok - lax.broadcasted_iota(jnp.int32, (1, cache_len), 1)
    ok_c = ((delta & (dilation - 1)) == 0) & (delta >= dilation) & (delta <= window)
    npos = lax.broadcasted_iota(jnp.int32, (1, SUBLANES), 1)
    dnew = tok - npos
    ok_n = (dnew >= 0) & ((dnew & (dilation - 1)) == 0) & (dnew <= window) & (npos < DEC_SEQ)
    lane_head = lax.broadcasted_iota(jnp.int32, (SUBLANES, A_W), 1) // A_DH
    qs = qpad * (A_DH ** -0.5)
    qs = jnp.concatenate([jnp.where(lane_head == h, qs, 0.0) for h in range(A_HEADS)], axis=0).astype(BF16)
    kt = cache_ref[0:A_W, :].astype(BF16)
    vt = cache_ref[A_W:2 * A_W, :].astype(BF16)
    s_c = jnp.where(ok_c, jnp.dot(qs, kt, preferred_element_type=F32), NEG_INF)
    s_n = jnp.where(ok_n, _nt_dot(qs, kvpad[:, 0:A_W].astype(BF16)), NEG_INF)
    mx = jnp.maximum(jnp.max(s_c, axis=-1, keepdims=True), jnp.max(s_n, axis=-1, keepdims=True))
    p_c = jnp.exp(s_c - mx)
    p_n = jnp.exp(s_n - mx)
    l = jnp.sum(p_c, axis=-1, keepdims=True) + jnp.sum(p_n, axis=-1, keepdims=True)
    inv = 1.0 / l
    pv = _nt_dot((p_c * inv).astype(BF16), vt) \
        + jnp.dot((p_n * inv).astype(BF16), kvpad[:, A_W:2 * A_W].astype(BF16), preferred_element_type=F32)
    lse = mx + jnp.log(l)
    o = jnp.zeros((SUBLANES, A_W), F32)
    le = jnp.zeros((SUBLANES, A_W), F32)
    for h in range(A_HEADS):
        rows = slice(h * SUBLANES, (h + 1) * SUBLANES)
        o = jnp.where(lane_head == h, pv[rows, :], o)
        le = jnp.where(lane_head == h, lse[rows, :], le)
    return o, le


def _attn_sample_kernel(q0_ref, kv0_ref, c0_ref, q1_ref, kv1_ref, c1_ref, q2_ref, kv2_ref, c2_ref,
                        o0_ref, l0_ref, o1_ref, l1_ref, o2_ref, l2_ref, qpad, kvpad):
    ins = ((q0_ref, kv0_ref, c0_ref, o0_ref, l0_ref), (q1_ref, kv1_ref, c1_ref, o1_ref, l1_ref),
           (q2_ref, kv2_ref, c2_ref, o2_ref, l2_ref))
    qpad[...] = jnp.zeros((SUBLANES, A_W), F32)
    kvpad[...] = jnp.zeros((SUBLANES, 2 * A_W), F32)
    for (window, dilation), (q_ref, kv_ref, c_ref, o_ref, l_ref) in zip(ATTN_GROUPS, ins):
        qpad[0:DEC_SEQ, :] = q_ref[...]
        kvpad[0:DEC_SEQ, :] = kv_ref[...]
        o, le = _attn_sample_group(qpad[...], kvpad[...], c_ref, window, dilation)
        o_ref[...] = o[0:DEC_SEQ, :]
        l_ref[...] = le[0:DEC_SEQ, :]


def _attn_sample(qs, kvs, caches):
    args, in_specs = [], []
    tok = lambda width: pl.BlockSpec((None, DEC_SEQ, width), lambda b: (b, 0, 0))
    for q, kv, cache in zip(qs, kvs, caches):
        cache_len = cache.shape[1]
        c3 = jnp.transpose(cache, (0, 2, 3, 4, 1)).reshape(DEC_BATCH, 2 * A_W, cache_len)
        args += [q.reshape(DEC_BATCH, DEC_SEQ, A_W), kv.reshape(DEC_BATCH, DEC_SEQ, 2 * A_W), c3]
        in_specs += [tok(A_W), tok(2 * A_W), pl.BlockSpec((None, 2 * A_W, cache_len), lambda b: (b, 0, 0))]
    outs = pl.pallas_call(
        _attn_sample_kernel,
        out_shape=[jax.ShapeDtypeStruct((DEC_BATCH, DEC_SEQ, A_W), F32)] * (2 * N_GROUPS),
        grid=(DEC_BATCH,),
        in_specs=in_specs,
        out_specs=[tok(A_W)] * (2 * N_GROUPS),
        scratch_shapes=[pltpu.VMEM((SUBLANES, A_W), F32), pltpu.VMEM((SUBLANES, 2 * A_W), F32)],
        compiler_params=pltpu.CompilerParams(vmem_limit_bytes=VMEM_LIMIT),
        name="attn_sample",
    )(*args)
    return [a.reshape(DEC_BATCH * DEC_SEQ, A_W) for a in outs]


def _merge_kernel(hm_ref, o0_ref, l0_ref, o1_ref, l1_ref, o2_ref, l2_ref, g_ref, x_ref, g1_ref, sc2_ref, sh2_ref,
                  wbm_ref, wba_ref, wo_ref, lng_ref, lnb_ref, wr_ref, br_ref, cnt0_ref,
                  x1_ref, h2_ref, ei_ref, eg_ref, er_ref, cnt_ref, run_s, scr, *, dils):
    i = pl.program_id(0)

    @pl.when(i == 0)
    def _():
        run_s[...] = cnt0_ref[...]

    l0, l1, l2 = [_load_dilated(r, scr, d) for r, d in zip((l0_ref, l1_ref, l2_ref), dils)]
    o0, o1, o2 = [_load_dilated(r, scr, d) for r, d in zip((o0_ref, o1_ref, o2_ref), dils)]
    mx = jnp.maximum(jnp.maximum(l0, l1), l2)
    e0, e1, e2 = jnp.exp(l0 - mx), jnp.exp(l1 - mx), jnp.exp(l2 - mx)
    inv = 1.0 / (e0 + e1 + e2)
    attn = (e0 * inv) * o0 + (e1 * inv) * o1 + (e2 * inv) * o2
    br_m = jnp.dot(hm_ref[...].astype(BF16), wbm_ref[...], preferred_element_type=F32)
    br_a = jnp.dot(attn.astype(BF16), wba_ref[...], preferred_element_type=F32)
    merged = g_ref[:, 0:D_MODEL] * br_m + g_ref[:, D_MODEL:2 * D_MODEL] * br_a
    mix = jnp.dot(merged.astype(BF16), wo_ref[...], preferred_element_type=F32)
    x1 = _ln(DN_ALPHA * x_ref[...] + g1_ref[...] * mix) * lng_ref[...] + lnb_ref[...]
    x1_ref[...] = x1
    h2 = _ln(x1) * (1.0 + sc2_ref[...]) + sh2_ref[...]
    h2_ref[...] = h2

    lane = lax.broadcasted_iota(jnp.int32, (TOK_TILE, LANES), 1)
    lane_f = lane.astype(F32)
    logits = jnp.dot(h2.astype(BF16), wr_ref[...], preferred_element_type=F32) + br_ref[...]
    cur = jnp.where(lane < N_EXPERTS, logits, NEG_INF)
    vals, onehots = [], []
    idx_mat = jnp.zeros((TOK_TILE, LANES), F32)
    for k in range(TOP_K):
        m = jnp.max(cur, axis=-1, keepdims=True)
        sel = jnp.min(jnp.where(cur == m, lane_f, float(LANES)), axis=-1, keepdims=True)
        hit = lane_f == sel
        vals.append(m)
        onehots.append(hit)
        idx_mat = jnp.where(lane == k, sel, idx_mat)
        cur = jnp.where(hit, NEG_INF, cur)
    es = [jnp.exp(v - vals[0]) for v in vals]
    tot = es[0] + es[1] + es[2] + es[3]
    gate_mat = jnp.zeros((TOK_TILE, LANES), F32)
    for k in range(TOP_K):
        gate_mat = jnp.where(lane == k, es[k] / tot, gate_mat)

    cnt = jnp.zeros((TOK_TILE, LANES), F32)
    for hit in onehots:
        cnt = cnt + hit.astype(F32)
    ri = lax.broadcasted_iota(jnp.int32, (TOK_TILE, TOK_TILE), 0)
    ci = lax.broadcasted_iota(jnp.int32, (TOK_TILE, TOK_TILE), 1)
    before = (ci < ri).astype(BF16)
    base = run_s[0:1, :] + jnp.dot(before, cnt.astype(BF16), preferred_element_type=F32)
    rank_mat = jnp.zeros((TOK_TILE, LANES), F32)
    for k, hit in enumerate(onehots):
        rk = jnp.sum(jnp.where(hit, base, 0.0), axis=-1, keepdims=True)
        rank_mat = jnp.where(lane == k, rk, rank_mat)
    new_run = run_s[0:1, :] + jnp.sum(cnt, axis=0, keepdims=True)
    run_s[...] = jnp.broadcast_to(new_run, (SUBLANES, LANES))
    cnt_ref[...] = run_s[...]
    ei_ref[...] = idx_mat[:, 0:TOP_K].astype(jnp.int32)
    eg_ref[...] = gate_mat[:, 0:TOP_K]
    er_ref[...] = rank_mat[:, 0:TOP_K].astype(jnp.int32)


def _merge(hm, attn_o, attn_l, gates, x2d, g1, sc2, sh2, wts, cnt0, per_token, tiles_per_batch, dils):
    n = x2d.shape[0]
    tok = lambda width, d=1: pl.BlockSpec((TOK_TILE // d, d * width), lambda i: (i, 0))
    const = lambda a: pl.BlockSpec(a.shape, lambda i: (0,) * a.ndim)
    mod = _mod_spec(per_token, tiles_per_batch)
    attn_args, attn_specs = [], []
    for o, l, d in zip(attn_o, attn_l, dils):
        attn_args += [o.reshape(n // d, d * A_W), l.reshape(n // d, d * A_W)]
        attn_specs += [tok(A_W, d), tok(A_W, d)]
    return pl.pallas_call(
        functools.partial(_merge_kernel, dils=dils),
        out_shape=[jax.ShapeDtypeStruct((n, D_MODEL), F32), jax.ShapeDtypeStruct((n, D_MODEL), F32),
                   jax.ShapeDtypeStruct((n, TOP_K), jnp.int32), jax.ShapeDtypeStruct((n, TOP_K), F32),
                   jax.ShapeDtypeStruct((n, TOP_K), jnp.int32), jax.ShapeDtypeStruct((SUBLANES, LANES), F32)],
        grid=(n // TOK_TILE,),
        in_specs=[tok(MV_W)] + attn_specs + [tok(2 * D_MODEL), tok(D_MODEL), mod, mod, mod]
                 + [const(w) for w in wts] + [const(cnt0)],
        out_specs=[tok(D_MODEL), tok(D_MODEL), tok(TOP_K), tok(TOP_K), tok(TOP_K),
                   pl.BlockSpec((SUBLANES, LANES), lambda i: (0, 0))],
        scratch_shapes=[pltpu.VMEM((SUBLANES, LANES), F32), pltpu.VMEM((A_W // LANES, TOK_TILE, LANES), F32)],
        compiler_params=pltpu.CompilerParams(dimension_semantics=("arbitrary",), vmem_limit_bytes=VMEM_LIMIT),
        name="merge_router",
    )(hm, *attn_args, gates, x2d, g1, sc2, sh2, *wts, cnt0)


def _row_copy(src_hbm, src_row, dst, dst_row, sem):
    return pltpu.make_async_copy(src_hbm.at[pl.ds(src_row, 1)], dst.at[pl.ds(dst_row, 1)], sem)


def _dispatch_kernel(pend_ref, dest_ref, h2p_ref, h2s_ref, xs_hbm, zeros, sem, *, n_prompt_tiles):
    i = pl.program_id(0)

    @pl.when(i == 0)
    def _():
        zeros[...] = jnp.zeros((MOE_TILE, D_MODEL), F32)
        n_blk = xs_hbm.shape[0] // MOE_TILE
        used_blk = pend_ref[N_EXPERTS - 1] // MOE_TILE

        def zero_block(start):
            return pltpu.make_async_copy(zeros, xs_hbm.at[pl.ds(pl.multiple_of(start, MOE_TILE), MOE_TILE)], sem)

        def last_block(e):
            return zero_block(jnp.maximum(pend_ref[e] - MOE_TILE, 0))

        def fill(e, carry):
            last_block(e).start()
            return carry

        def fill_done(e, carry):
            last_block(e).wait()
            return carry

        def tail(b, carry):
            zero_block(b * MOE_TILE).start()
            return carry

        def tail_done(b, carry):
            zero_block(b * MOE_TILE).wait()
            return carry

        lax.fori_loop(0, N_EXPERTS, fill, 0)
        lax.fori_loop(used_blk, n_blk, tail, 0)
        lax.fori_loop(0, N_EXPERTS, fill_done, 0)
        lax.fori_loop(used_blk, n_blk, tail_done, 0)

    def scatter(h2_ref):
        def issue(t, carry):
            for k in range(TOP_K):
                _row_copy(h2_ref, t, xs_hbm, dest_ref[t * TOP_K + k], sem).start(priority=k % 2)
            return carry

        def drain(j, carry):
            _row_copy(h2_ref, 0, xs_hbm, 0, sem).wait()
            return carry

        lax.fori_loop(0, TOK_TILE, issue, 0)
        lax.fori_loop(0, TOK_TILE * TOP_K, drain, 0, unroll=8)

    @pl.when(i < n_prompt_tiles)
    def _():
        scatter(h2p_ref)

    @pl.when(i >= n_prompt_tiles)
    def _():
        scatter(h2s_ref)


def _dispatch(pend, dest_flat, h2_p, h2_s, n_rows):
    tp, ts = h2_p.shape[0] // TOK_TILE, h2_s.shape[0] // TOK_TILE
    return pl.pallas_call(
        functools.partial(_dispatch_kernel, n_prompt_tiles=tp),
        out_shape=jax.ShapeDtypeStruct((n_rows, D_MODEL), F32),
        grid_spec=pltpu.PrefetchScalarGridSpec(
            num_scalar_prefetch=1, grid=(tp + ts,),
            in_specs=[pl.BlockSpec((TOK_TILE * TOP_K,), lambda i, pe: (i,), memory_space=pltpu.SMEM),
                      pl.BlockSpec((TOK_TILE, D_MODEL), lambda i, pe: (jnp.minimum(i, tp - 1), 0)),
                      pl.BlockSpec((TOK_TILE, D_MODEL), lambda i, pe: (jnp.maximum(i - tp, 0), 0))],
            out_specs=pl.BlockSpec(memory_space=pl.ANY),
            scratch_shapes=[pltpu.VMEM((MOE_TILE, D_MODEL), F32), pltpu.SemaphoreType.DMA(())]),
        compiler_params=pltpu.CompilerParams(dimension_semantics=("arbitrary",)),
        name="moe_dispatch",
    )(pend, dest_flat, h2_p, h2_s)


CAST_ROWS = 128


def _expert_kernel(be_ref, bv_ref, xs_ref, wu_ref, bu_ref, wd_ref, bd_ref, ys_ref, wu_bf, wd_bf):
    i = pl.program_id(0)

    @pl.when((i == 0) | (be_ref[i] != be_ref[jnp.maximum(i - 1, 0)]))
    def _():
        def cast(r, carry):
            rows = pl.ds(pl.multiple_of(r * CAST_ROWS, CAST_ROWS), CAST_ROWS)
            wu_bf[rows, :] = wu_ref[rows, :].astype(BF16)
            wd_bf[rows, :] = wd_ref[rows, :].astype(BF16)
            return carry

        lax.fori_loop(0, D_MODEL // CAST_ROWS, cast, 0)

    @pl.when(bv_ref[i] == 1)
    def _():
        hu = jnp.dot(xs_ref[...].astype(BF16), wu_bf[...], preferred_element_type=F32) + bu_ref[...]
        x_glu = jnp.minimum(hu[:, 0:D_FF], SWIGLU_LIMIT)
        x_lin = jnp.clip(hu[:, D_FF:2 * D_FF], -SWIGLU_LIMIT, SWIGLU_LIMIT)
        act = x_glu * _sigmoid(SWIGLU_ALPHA * x_glu) * (x_lin + 1.0)
        ys_ref[...] = jnp.dot(act.astype(BF16), wd_bf[...], preferred_element_type=F32) + bd_ref[...]

    @pl.when(bv_ref[i] == 0)
    def _():
        ys_ref[...] = jnp.zeros((MOE_TILE, D_MODEL), F32)


def _experts(blk_e, blk_valid, xs, w_up, b_up, w_down, b_down):
    assert D_FF == D_MODEL
    n_blk = blk_e.shape[0]
    return pl.pallas_call(
        _expert_kernel,
        out_shape=jax.ShapeDtypeStruct(xs.shape, F32),
        grid_spec=pltpu.PrefetchScalarGridSpec(
            num_scalar_prefetch=2,
            grid=(n_blk,),
            in_specs=[pl.BlockSpec((MOE_TILE, D_MODEL), lambda i, be, bv: (i * bv[i], 0)),
                      pl.BlockSpec((None, D_MODEL, 2 * D_FF), lambda i, be, bv: (be[i], 0, 0)),
                      pl.BlockSpec((None, 1, 2 * D_FF), lambda i, be, bv: (be[i], 0, 0)),
                      pl.BlockSpec((None, D_FF, D_MODEL), lambda i, be, bv: (be[i], 0, 0)),
                      pl.BlockSpec((None, 1, D_MODEL), lambda i, be, bv: (be[i], 0, 0))],
            out_specs=pl.BlockSpec((MOE_TILE, D_MODEL), lambda i, be, bv: (i, 0)),
            scratch_shapes=[pltpu.VMEM((D_MODEL, 2 * D_FF), BF16), pltpu.VMEM((D_FF, D_MODEL), BF16)]),
        compiler_params=pltpu.CompilerParams(dimension_semantics=("arbitrary",), vmem_limit_bytes=VMEM_LIMIT),
        name="moe_experts",
    )(blk_e, blk_valid, xs, w_up, b_up.reshape(N_EXPERTS, 1, -1), w_down, b_down.reshape(N_EXPERTS, 1, -1))


def _combine_kernel(dest_ref, ys_hbm, gate_ref, x1_ref, g2_ref, lng_ref, lnb_ref, out_ref, rows, sem):
    def issue(t, carry):
        for k in range(TOP_K):
            _row_copy(ys_hbm, dest_ref[t * TOP_K + k], rows.at[k], t, sem).start(priority=k % 2)
        return carry

    def drain(j, carry):
        _row_copy(ys_hbm, 0, rows.at[0], 0, sem).wait()
        return carry

    lax.fori_loop(0, TOK_TILE, issue, 0)
    lax.fori_loop(0, TOK_TILE * TOP_K, drain, 0, unroll=8)
    gate = gate_ref[...]
    f = gate[:, 0:1] * rows[0]
    for k in range(1, TOP_K):
        f = f + gate[:, k:k + 1] * rows[k]
    out_ref[...] = _ln(DN_ALPHA * x1_ref[...] + g2_ref[...] * f) * lng_ref[...] + lnb_ref[...]


def _combine(dest_flat, ys, gate, x1, g2, ln_g, ln_b, per_token, tiles_per_batch):
    n = x1.shape[0]
    tok = lambda width: pl.BlockSpec((TOK_TILE, width), lambda i: (i, 0))
    const = pl.BlockSpec((1, D_MODEL), lambda i: (0, 0))
    return pl.pallas_call(
        _combine_kernel,
        out_shape=jax.ShapeDtypeStruct((n, D_MODEL), F32),
        grid=(n // TOK_TILE,),
        in_specs=[pl.BlockSpec((TOK_TILE * TOP_K,), lambda i: (i,), memory_space=pltpu.SMEM),
                  pl.BlockSpec(memory_space=pl.ANY), tok(TOP_K), tok(D_MODEL),
                  _mod_spec(per_token, tiles_per_batch), const, const],
        out_specs=tok(D_MODEL),
        scratch_shapes=[pltpu.VMEM((TOP_K, TOK_TILE, D_MODEL), F32), pltpu.SemaphoreType.DMA(())],
        compiler_params=pltpu.CompilerParams(dimension_semantics=("arbitrary",), vmem_limit_bytes=VMEM_LIMIT),
        name="moe_combine",
    )(dest_flat, ys, gate, x1, g2, ln_g.reshape(1, -1), ln_b.reshape(1, -1))


def _routing(e_idx, rank, counts):
    n_assign = e_idx.shape[0] * TOP_K
    n_blk = n_assign // MOE_TILE + N_EXPERTS
    padded = (counts + MOE_TILE - 1) // MOE_TILE * MOE_TILE
    pend = jnp.cumsum(padded)
    pstart = pend - padded
    dest = pstart[e_idx] + rank
    blk_row = jnp.arange(n_blk, dtype=jnp.int32) * MOE_TILE
    blk_e = jnp.minimum(jnp.sum((pend[None, :] <= blk_row[:, None]).astype(jnp.int32), axis=1), N_EXPERTS - 1)
    blk_valid = (blk_row < pend[-1]).astype(jnp.int32)
    return dest.reshape(-1).astype(jnp.int32), pend.astype(jnp.int32), blk_e, blk_valid, n_blk * MOE_TILE


def kernel(x_prompt, x_sample, c_prompt, c_sample, cache_kv_w128, cache_kv_w512, cache_kv_w2048,
           state_mlstm_C, state_mlstm_n, state_mlstm_m, state_qk_conv,
           w_ada, b_ada, w_in, b_in, conv_w, conv_b, mh_norm_w, w_br_mlstm, w_br_attn,
           w_merge, b_merge, w_o, ln1_g, ln1_b, w_router, b_router, w_up, b_up, w_down, b_down,
           ln2_g, ln2_b):
    assert DEPTH == 1 and SEQ % (ATTN_GROUPS[-1][0]) == 0 and SEQ % MLSTM_CHUNK == 0
    n_p, n_s = BATCH * SEQ, DEC_BATCH * DEC_SEQ
    tiles_per_batch = SEQ // TOK_TILE
    layer = 0

    wi, bi = w_in[layer], b_in[layer]
    pad_if = LANES - 2 * M_HEADS
    w1 = jnp.concatenate([wi[:, :OFF_V], wi[:, OFF_V:OFF_I], wi[:, OFF_O:OFF_A], wi[:, OFF_A:], w_merge[layer],
                          wi[:, OFF_I:OFF_O], jnp.zeros((D_MODEL, pad_if), F32)], axis=1).astype(BF16)
    b1 = jnp.concatenate([bi[:OFF_V], bi[OFF_V:OFF_I], bi[OFF_O:OFF_A], bi[OFF_A:], b_merge[layer],
                          bi[OFF_I:OFF_O], jnp.zeros((pad_if,), F32)]).reshape(1, -1)
    wr = jnp.pad(w_router[layer], ((0, 0), (0, LANES - N_EXPERTS))).astype(BF16)
    br = jnp.pad(b_router[layer], (0, LANES - N_EXPERTS)).reshape(1, -1)
    merge_w = (w_br_mlstm[layer].astype(BF16), w_br_attn[layer].astype(BF16), w_o[layer].astype(BF16),
               ln1_g[layer].reshape(1, -1), ln1_b[layer].reshape(1, -1), wr, br)

    mod = _ada(jnp.concatenate([c_prompt, c_sample], axis=0), w_ada[layer], b_ada[layer])
    mod_p = mod[:BATCH].reshape(BATCH, 6, 1, D_MODEL)
    mod_s = jnp.repeat(mod[BATCH:], DEC_SEQ, axis=0).reshape(n_s, 6, D_MODEL)
    mods_p = [mod_p[:, j] for j in range(6)]
    mods_s = [mod_s[:, j] for j in range(6)]

    results = {}
    cnt = jnp.zeros((SUBLANES, LANES), F32)
    stage = []
    for name, x, mods, per_token in (("p", x_prompt, mods_p, False), ("s", x_sample, mods_s, True)):
        x2d = x.reshape(-1, D_MODEL)
        dils = tuple(d for _, d in ATTN_GROUPS) if name == "p" else (1,) * N_GROUPS
        qk, v, o, q0, kv0, q1, kv1, q2, kv2, gates, ifg = _inproj(x2d, mods[1], mods[0], w1, b1, per_token,
                                                                    tiles_per_batch, dils)
        qs, kvs = (q0, q1, q2), (kv0, kv1, kv2)
        if name == "p":
            zeros = lambda *s: jnp.zeros(s, F32)
            hm, c_new, n_new, m_new, hist = _mlstm(
                qk, v, o, ifg, conv_w[layer], conv_b[layer], mh_norm_w[layer],
                zeros(BATCH, SUBLANES, QK_W), zeros(BATCH, M_HEADS, M_DQK, M_DV), zeros(BATCH, SUBLANES, M_DQK),
                zeros(BATCH, SUBLANES, LANES), n_seq=BATCH, n_chunks=SEQ // MLSTM_CHUNK, chunk=MLSTM_CHUNK,
                valid=MLSTM_CHUNK, n_par=1)
            attn, new_kv = [], []
            for q, kv, (w, dil) in zip(qs, kvs, ATTN_GROUPS):
                kv3 = kv.reshape(BATCH, SEQ // dil, dil * 2 * A_W)
                attn.append(_attn_prompt(q.reshape(BATCH, SEQ // dil, dil * A_W), kv3, dil))
                win = min(w, SEQ)
                new_kv.append(kv3[:, (SEQ - win) // dil:].reshape(BATCH, win, 2, A_HEADS, A_DH))
            attn_o, attn_l = [a[0] for a in attn], [a[1] for a in attn]
            nb = BATCH
        else:
            m0 = jnp.broadcast_to(state_mlstm_m[layer][:, :, None], (DEC_BATCH, M_HEADS, LANES))
            hm, c_new, n_new, m_new, hist = _mlstm(
                qk, v, o, ifg, conv_w[layer], conv_b[layer], mh_norm_w[layer],
                _pad_rows(state_qk_conv[layer], SUBLANES, True), state_mlstm_C[layer],
                _pad_rows(state_mlstm_n[layer], SUBLANES, False), _pad_rows(m0, SUBLANES, False),
                n_seq=DEC_BATCH, n_chunks=1, chunk=SUBLANES, valid=DEC_SEQ, n_par=MLSTM_SAMPLE_PAR)
            caches = (cache_kv_w128[layer], cache_kv_w512[layer], cache_kv_w2048[layer])
            outs = _attn_sample(qs, kvs, caches)
            attn_o, attn_l = outs[0::2], outs[1::2]
            new_kv = [kv.reshape(DEC_BATCH, DEC_SEQ, 2, A_HEADS, A_DH) for kv in kvs]
            nb = DEC_BATCH
        x1, h2, e_idx, e_gate, e_rank, cnt_new = _merge(hm, attn_o, attn_l, gates, x2d, mods[2], mods[4], mods[3],
                                                         merge_w, cnt, per_token, tiles_per_batch, dils)
        results[name] = dict(
            x1=x1, h2=h2, e_idx=e_idx, e_gate=e_gate, e_rank=e_rank, g2=mods[5], per_token=per_token,
            states=(new_kv[0][None], new_kv[1][None], new_kv[2][None], c_new[None],
                    n_new[:, :M_HEADS][None], m_new[:, :M_HEADS, 0][None],
                    hist[:, SUBLANES - (CONV_W - 1):][None]), shape=x.shape)
        cnt = cnt_new
        stage.append(name)

    e_idx = jnp.concatenate([results["p"]["e_idx"], results["s"]["e_idx"]], axis=0)
    e_rank = jnp.concatenate([results["p"]["e_rank"], results["s"]["e_rank"]], axis=0)
    counts = cnt[0, :N_EXPERTS].astype(jnp.int32)
    dest, pend, blk_e, blk_valid, n_rows = _routing(e_idx, e_rank, counts)
    xs = _dispatch(pend, dest, results["p"]["h2"], results["s"]["h2"], n_rows)
    ys = _experts(blk_e, blk_valid, xs, w_up[layer], b_up[layer], w_down[layer], b_down[layer])
    ys_out = {}
    off = 0
    for name in ("p", "s"):
        r = results[name]
        n = r["x1"].shape[0]
        ys_out[name] = _combine(dest[off * TOP_K:(off + n) * TOP_K], ys, r["e_gate"], r["x1"],
                                r["g2"], ln2_g[layer], ln2_b[layer], r["per_token"], tiles_per_batch
                                ).reshape(r["shape"])
        off += n
    return (ys_out["p"], ys_out["s"]) + results["p"]["states"] + results["s"]["states"]
```

```python
import functools

import jax
import jax.numpy as jnp
from jax import lax
from jax.experimental import pallas as pl
from jax.experimental.pallas import tpu as pltpu

D_MODEL = 1024
BATCH = 2
SEQ = 8192
DEPTH = 1
DEC_BATCH = 128
DEC_SEQ = 4
PAST_LEN = 2048

M_HEADS = 4
M_DQK = D_MODEL // 8
M_DV = D_MODEL // 4
CONV_W = 4
ATTN_GROUPS = ((128, 1), (512, 4), (2048, 16))
A_HEADS = 4
A_DH = D_MODEL // 16
N_EXPERTS = 32
TOP_K = 4
D_FF = D_MODEL
SWIGLU_LIMIT = 7.0
SWIGLU_ALPHA = 1.702
DN_ALPHA = (2 * DEPTH) ** 0.25
LN_EPS = 1e-5

QK_W = 2 * M_HEADS * M_DQK
MV_W = M_HEADS * M_DV
A_W = A_HEADS * A_DH
OFF_V = QK_W
OFF_I = OFF_V + MV_W
OFF_F = OFF_I + M_HEADS
OFF_O = OFF_F + M_HEADS
OFF_A = OFF_O + MV_W
N_GROUPS = len(ATTN_GROUPS)
P_W = OFF_A + 3 * A_W * N_GROUPS

LANES = 128
SUBLANES = 8
ATTN_BLK = 128
MLSTM_CHUNK = 128
MLSTM_SAMPLE_PAR = 4
ATTN_SAMPLE_PAR = 2
ISSUE_UNROLL = 4
TOK_TILE = 256
MOE_TILE = 512
VMEM_LIMIT = 56 * 1024 * 1024

SEG_QK = 0
SEG_V = SEG_QK + QK_W
SEG_O = SEG_V + MV_W
SEG_A = SEG_O + MV_W
SEG_G = SEG_A + 3 * A_W * N_GROUPS
SEG_IF = SEG_G + 2 * D_MODEL
W1_COLS = SEG_IF + LANES

F32 = jnp.float32
BF16 = jnp.bfloat16
NEG_INF = float("-inf")


def _ln(x):
    mu = jnp.mean(x, axis=-1, keepdims=True)
    xc = x - mu
    var = jnp.mean(xc * xc, axis=-1, keepdims=True)
    return xc * lax.rsqrt(var + LN_EPS)


def _sigmoid(x):
    return 1.0 / (1.0 + jnp.exp(-x))


def _nt_dot(a, b, precision=None):
    return lax.dot_general(a, b, (((1,), (1,)), ((), ())), preferred_element_type=F32, precision=precision)


def _tn_dot(a, b):
    return lax.dot_general(a, b, (((0,), (0,)), ((), ())), preferred_element_type=F32)


def _ada_kernel(c_ref, w_ref, b_ref, o_ref):
    c = c_ref[...]
    s = c * _sigmoid(c)
    o_ref[...] = jnp.dot(s.astype(BF16), w_ref[...].astype(BF16), preferred_element_type=F32) + b_ref[...]


def _ada(c_all, w_ada, b_ada):
    r = c_all.shape[0]
    n_col = 6 * D_MODEL // D_MODEL
    return pl.pallas_call(
        _ada_kernel,
        out_shape=jax.ShapeDtypeStruct((r, 6 * D_MODEL), F32),
        grid=(n_col,),
        in_specs=[pl.BlockSpec((r, D_MODEL), lambda j: (0, 0)),
                  pl.BlockSpec((D_MODEL, D_MODEL), lambda j: (0, j)),
                  pl.BlockSpec((1, D_MODEL), lambda j: (0, j))],
        out_specs=pl.BlockSpec((r, D_MODEL), lambda j: (0, j)),
        name="ada_mod",
    )(c_all, w_ada, b_ada.reshape(1, -1))


def _store_dilated(out_ref, val, scr, dil):
    w = val.shape[1]
    if dil == 1:
        out_ref[...] = val
        return
    for c in range(w // LANES):
        scr[c] = val[:, c * LANES:(c + 1) * LANES]
    for r in range(dil):
        for c in range(w // LANES):
            out_ref[:, r * w + c * LANES:r * w + (c + 1) * LANES] = scr[c, pl.ds(r, TOK_TILE // dil, stride=dil), :]


def _load_dilated(in_ref, scr, dil):
    if dil == 1:
        return in_ref[...]
    w = in_ref.shape[1] // dil
    for r in range(dil):
        for c in range(w // LANES):
            scr[c, pl.ds(r, TOK_TILE // dil, stride=dil), :] = in_ref[:, r * w + c * LANES:r * w + (c + 1) * LANES]
    return jnp.concatenate([scr[c] for c in range(w // LANES)], axis=-1)


def _inproj_kernel(x_ref, sc_ref, sh_ref, w_ref, b_ref,
                   qk_ref, v_ref, o_ref, q0_ref, kv0_ref, q1_ref, kv1_ref, q2_ref, kv2_ref, g_ref, if_ref, scr,
                   *, dils):
    h = _ln(x_ref[...]) * (1.0 + sc_ref[...]) + sh_ref[...]
    hb = h.astype(BF16)

    def seg(a, width):
        return jnp.dot(hb, w_ref[:, a:a + width], preferred_element_type=F32) + b_ref[:, a:a + width]

    qk_ref[...] = seg(SEG_QK, QK_W)
    v_ref[...] = seg(SEG_V, MV_W)
    o_ref[...] = seg(SEG_O, MV_W)
    for gi, (q_ref, kv_ref) in enumerate(((q0_ref, kv0_ref), (q1_ref, kv1_ref), (q2_ref, kv2_ref))):
        base = SEG_A + 3 * A_W * gi
        _store_dilated(q_ref, seg(base, A_W), scr, dils[gi])
        _store_dilated(kv_ref, seg(base + A_W, 2 * A_W), scr, dils[gi])
    g_ref[...] = _sigmoid(seg(SEG_G, 2 * D_MODEL))
    if_ref[...] = seg(SEG_IF, LANES)


MOD_SHIFT1, MOD_SCALE1, MOD_GATE1, MOD_SHIFT2, MOD_SCALE2, MOD_GATE2 = range(6)


def _mod_spec(per_token, tiles_per_batch, j):
    if per_token:
        return pl.BlockSpec((TOK_TILE, D_MODEL), lambda i: (i, j))
    return pl.BlockSpec((None, 1, D_MODEL), lambda i: (i // tiles_per_batch, 0, j))


def _inproj(x2d, mod, w1, b1, per_token, tiles_per_batch, dils):
    n = x2d.shape[0]
    tok = lambda width, d=1: pl.BlockSpec((TOK_TILE // d, d * width), lambda i: (i, 0))
    shape = lambda width, d=1: jax.ShapeDtypeStruct((n // d, d * width), F32)
    layout = [(QK_W, 1), (MV_W, 1), (MV_W, 1)]
    for d in dils:
        layout += [(A_W, d), (2 * A_W, d)]
    layout += [(2 * D_MODEL, 1), (LANES, 1)]
    return pl.pallas_call(
        functools.partial(_inproj_kernel, dils=dils),
        out_shape=[shape(w, d) for w, d in layout],
        grid=(n // TOK_TILE,),
        in_specs=[tok(D_MODEL), _mod_spec(per_token, tiles_per_batch, MOD_SCALE1),
                  _mod_spec(per_token, tiles_per_batch, MOD_SHIFT1),
                  pl.BlockSpec((D_MODEL, W1_COLS), lambda i: (0, 0), pipeline_mode=pl.Buffered(1)),
                  pl.BlockSpec((1, W1_COLS), lambda i: (0, 0))],
        out_specs=[tok(w, d) for w, d in layout],
        scratch_shapes=[pltpu.VMEM((2 * A_W // LANES, TOK_TILE, LANES), F32)],
        compiler_params=pltpu.CompilerParams(vmem_limit_bytes=VMEM_LIMIT),
        name="inproj",
    )(x2d, mod, mod, w1, b1)


def _mlstm_kernel(qk_ref, v_ref, o_ref, if_ref, cw_ref, cb_ref, nw_ref, *rest, chunk, valid, n_par):
    for s in range(n_par):
        _mlstm_seq(qk_ref.at[s], v_ref.at[s], o_ref.at[s], if_ref.at[s], cw_ref, cb_ref, nw_ref,
                   *[r.at[s] for r in rest], chunk=chunk, valid=valid)


def _mlstm_seq(qk_ref, v_ref, o_ref, if_ref, cw_ref, cb_ref, nw_ref, hist0_ref, c0_ref, n0_ref, m0_ref,
               hm_ref, cout_ref, nout_ref, mout_ref, hout_ref,
               xbuf, vbuf, obuf, gbuf, c_s, n_s, m_s, *, chunk, valid):
    c_idx = pl.program_id(1)
    last = pl.num_programs(1) - 1

    @pl.when(c_idx == 0)
    def _():
        xbuf[0:SUBLANES, :] = hist0_ref[...]
        c_s[...] = c0_ref[...]
        n_s[...] = n0_ref[...]
        m_s[...] = m0_ref[...]

    if valid < chunk:
        xbuf[SUBLANES:SUBLANES + chunk, :] = jnp.zeros((chunk, QK_W), F32)
        vbuf[...] = jnp.zeros((chunk, MV_W), F32)
        obuf[...] = jnp.zeros((chunk, MV_W), F32)
        gbuf[...] = jnp.zeros((chunk, LANES), F32)
        xbuf[SUBLANES:SUBLANES + valid, :] = qk_ref[...]
        vbuf[0:valid, :] = v_ref[...]
        obuf[0:valid, :] = o_ref[...]
        gbuf[0:valid, :] = if_ref[...]
        v_all, o_all, g_raw = vbuf[...], obuf[...], gbuf[...]
    else:
        xbuf[SUBLANES:SUBLANES + chunk, :] = qk_ref[...]
        v_all, o_all, g_raw = v_ref[...], o_ref[...], if_ref[...]

    conv = cb_ref[...]
    for j in range(CONV_W):
        off = SUBLANES - (CONV_W - 1) + j
        conv = conv + cw_ref[j:j + 1, :] * xbuf[off:off + chunk, :]
    new_hist = xbuf[valid:valid + SUBLANES, :]
    xbuf[0:SUBLANES, :] = new_hist
    act = conv * _sigmoid(conv)
    if valid < chunk:
        rows = lax.broadcasted_iota(jnp.int32, (chunk, 1), 0)
        act = jnp.where(rows < valid, act, 0.0)

    lane = lax.broadcasted_iota(jnp.int32, (chunk, LANES), 1)
    is_f = (lane >= M_HEADS) & (lane < 2 * M_HEADS)
    lf = jnp.minimum(g_raw, 0.0) - jnp.log(1.0 + jnp.exp(-jnp.abs(g_raw)))
    g_lin = jnp.where(is_f, lf, g_raw)
    if valid < chunk:
        rows_l = lax.broadcasted_iota(jnp.int32, (chunk, LANES), 0)
        g_lin = jnp.where((rows_l >= valid) & is_f, 0.0, g_lin)
    ri = lax.broadcasted_iota(jnp.int32, (chunk, chunk), 0)
    ci = lax.broadcasted_iota(jnp.int32, (chunk, chunk), 1)
    causal = ci <= ri
    tri = causal.astype(F32)
    csum = jnp.dot(tri, g_lin, preferred_element_type=F32, precision=lax.Precision.HIGHEST)
    m_col = jnp.where(is_f, csum, g_lin)
    sel = (lax.broadcasted_iota(jnp.int32, (SUBLANES, LANES), 0)
           == lax.broadcasted_iota(jnp.int32, (SUBLANES, LANES), 1)).astype(F32)
    m_row = _nt_dot(sel, m_col, precision=lax.Precision.HIGHEST)
    if valid < chunk:
        tcol = lax.broadcasted_iota(jnp.int32, (chunk, 1), 0)
        trow = lax.broadcasted_iota(jnp.int32, (1, chunk), 1)

    for h in range(M_HEADS):
        q = act[:, h * M_DQK:(h + 1) * M_DQK]
        k = act[:, QK_W // 2 + h * M_DQK:QK_W // 2 + (h + 1) * M_DQK] * (M_DQK ** -0.5)
        v = v_all[:, h * M_DV:(h + 1) * M_DV]
        qb, kb, vb = q.astype(BF16), k.astype(BF16), v.astype(BF16)
        li_col = m_col[:, h:h + 1]
        b_col = m_col[:, M_HEADS + h:M_HEADS + h + 1]
        li_row = m_row[h:h + 1, :]
        b_row = m_row[M_HEADS + h:M_HEADS + h + 1, :]
        if valid < chunk:
            li_col = jnp.where(tcol < valid, li_col, NEG_INF)
            li_row = jnp.where(trow < valid, li_row, NEG_INF)
        m_prev = m_s[h:h + 1, 0:1]
        c_prev = c_s[h]
        n_prev = n_s[h:h + 1, :]

        dmat = jnp.where(causal, b_col - b_row + li_row, NEG_INF)
        inter = b_col + m_prev
        m_t = jnp.maximum(inter, jnp.max(dmat, axis=-1, keepdims=True))
        s_qk = _nt_dot(qb, kb)
        a = jnp.exp(dmat - m_t) * s_qk
        sc = jnp.exp(inter - m_t)
        num = sc * jnp.dot(qb, c_prev.astype(BF16), preferred_element_type=F32) \
            + jnp.dot(a.astype(BF16), vb, preferred_element_type=F32)
        den = sc * jnp.sum(q * n_prev, axis=-1, keepdims=True) + jnp.sum(a, axis=-1, keepdims=True)
        hh = num / jnp.maximum(jnp.abs(den), jnp.exp(-m_t))

        b_end = b_col[chunk - 1:chunk, :]
        g_col = b_end - b_col + li_col
        g_row = b_end - b_row + li_row
        m_new = jnp.maximum(b_end + m_prev, jnp.max(g_row, axis=-1, keepdims=True))
        w_col = jnp.exp(g_col - m_new)
        decay = jnp.exp(b_end + m_prev - m_new)
        c_s[h] = decay * c_prev + _tn_dot(kb, (w_col * v).astype(BF16))
        n_s[h:h + 1, :] = decay * n_prev + jnp.sum(w_col * k, axis=0, keepdims=True)
        m_s[h:h + 1, :] = jnp.broadcast_to(m_new, (1, LANES))

        y = _ln(hh) * nw_ref[:, h * M_DV:(h + 1) * M_DV] * _sigmoid(o_all[:, h * M_DV:(h + 1) * M_DV])
        hm_ref[:, h * M_DV:(h + 1) * M_DV] = y[0:valid, :]

    @pl.when(c_idx == last)
    def _():
        cout_ref[...] = c_s[...]
        nout_ref[...] = n_s[...]
        mout_ref[...] = m_s[...]
        hout_ref[...] = xbuf[0:SUBLANES, :]


def _mlstm(qk, v, o, ifg, conv_w, conv_b, norm_w, hist0, c0, n0, m0, *, n_seq, n_chunks, chunk, valid, n_par):
    seq_len = n_chunks * valid
    tok = lambda width: pl.BlockSpec((n_par, valid, width), lambda b, c: (b, c, 0))
    args = tuple(a.reshape(n_seq, seq_len, a.shape[-1]) for a in (qk, v, o, ifg))
    const2 = lambda r, w: pl.BlockSpec((r, w), lambda b, c: (0, 0))
    per_seq = lambda *dims: pl.BlockSpec((n_par,) + dims, lambda b, c: (b,) + (0,) * len(dims))
    par = lambda *dims: pltpu.VMEM((n_par,) + dims, F32)
    outs = pl.pallas_call(
        functools.partial(_mlstm_kernel, chunk=chunk, valid=valid, n_par=n_par),
        out_shape=[jax.ShapeDtypeStruct((n_seq, seq_len, MV_W), F32),
                   jax.ShapeDtypeStruct((n_seq, M_HEADS, M_DQK, M_DV), F32),
                   jax.ShapeDtypeStruct((n_seq, SUBLANES, M_DQK), F32),
                   jax.ShapeDtypeStruct((n_seq, SUBLANES, LANES), F32),
                   jax.ShapeDtypeStruct((n_seq, SUBLANES, QK_W), F32)],
        grid=(n_seq // n_par, n_chunks),
        in_specs=[tok(QK_W), tok(MV_W), tok(MV_W), tok(LANES),
                  const2(CONV_W, QK_W), const2(1, QK_W), const2(1, MV_W),
                  per_seq(SUBLANES, QK_W), per_seq(M_HEADS, M_DQK, M_DV), per_seq(SUBLANES, M_DQK),
                  per_seq(SUBLANES, LANES)],
        out_specs=[tok(MV_W), per_seq(M_HEADS, M_DQK, M_DV), per_seq(SUBLANES, M_DQK),
                   per_seq(SUBLANES, LANES), per_seq(SUBLANES, QK_W)],
        scratch_shapes=[par(chunk + 2 * SUBLANES, QK_W), par(chunk, MV_W), par(chunk, MV_W), par(chunk, LANES),
                        par(M_HEADS, M_DQK, M_DV), par(SUBLANES, M_DQK), par(SUBLANES, LANES)],
        compiler_params=pltpu.CompilerParams(dimension_semantics=("arbitrary", "arbitrary"),
                                             vmem_limit_bytes=VMEM_LIMIT),
        name="mlstm",
    )(*args, conv_w, conv_b.reshape(1, -1), norm_w.reshape(1, -1), hist0, c0, n0, m0)
    hm = outs[0].reshape(n_seq * seq_len, MV_W)
    return hm, outs[1], outs[2], outs[3], outs[4]


def _pad_rows(a, rows, at_end):
    pad = rows - a.shape[1]
    cfg = ((0, 0), (pad, 0), (0, 0)) if at_end else ((0, 0), (0, pad), (0, 0))
    return jnp.pad(a, cfg)


ATTN_QBLKS = 2


def _attn_block(q, kc, vc, kp, vp, prev_ok):
    qi = lax.broadcasted_iota(jnp.int32, (ATTN_BLK, ATTN_BLK), 0)
    ki = lax.broadcasted_iota(jnp.int32, (ATTN_BLK, ATTN_BLK), 1)
    mask_c = ki <= qi
    mask_p = (ki >= qi) & prev_ok
    lane_head = lax.broadcasted_iota(jnp.int32, (ATTN_BLK, A_W), 1) // A_DH
    q = q * (A_DH ** -0.5)
    qs = jnp.concatenate([jnp.where(lane_head == h, q, 0.0) for h in range(A_HEADS)], axis=0).astype(BF16)
    mask_c = jnp.concatenate([mask_c] * A_HEADS, axis=0)
    mask_p = jnp.concatenate([mask_p] * A_HEADS, axis=0)
    s_c = jnp.where(mask_c, _nt_dot(qs, kc), NEG_INF)
    s_p = jnp.where(mask_p, _nt_dot(qs, kp), NEG_INF)
    mx = jnp.max(jnp.maximum(s_c, s_p), axis=-1, keepdims=True)
    p_c = jnp.exp(s_c - mx)
    p_p = jnp.exp(s_p - mx)
    l = jnp.sum(p_c + p_p, axis=-1, keepdims=True)
    inv = 1.0 / l
    pv = jnp.dot((p_c * inv).astype(BF16), vc, preferred_element_type=F32) \
        + jnp.dot((p_p * inv).astype(BF16), vp, preferred_element_type=F32)
    lse = mx + jnp.log(l)
    o = jnp.zeros((ATTN_BLK, A_W), F32)
    le = jnp.zeros((ATTN_BLK, A_W), F32)
    for h in range(A_HEADS):
        rows = slice(h * ATTN_BLK, (h + 1) * ATTN_BLK)
        o = jnp.where(lane_head == h, pv[rows, :], o)
        le = jnp.where(lane_head == h, lse[rows, :], le)
    return o, le


def _attn_prompt_kernel(q_ref, kvc_ref, kvp_ref, o_ref, l_ref):
    n = pl.program_id(2)
    for j in range(ATTN_QBLKS):
        rows = slice(j * ATTN_BLK, (j + 1) * ATTN_BLK)
        if j == 0:
            prev_ref, prows, prev_ok = kvp_ref, slice(0, ATTN_BLK), n > 0
        else:
            prev_ref, prows, prev_ok = kvc_ref, slice((j - 1) * ATTN_BLK, j * ATTN_BLK), True
        o, le = _attn_block(q_ref[rows, :],
                            kvc_ref[rows, 0:A_W].astype(BF16), kvc_ref[rows, A_W:2 * A_W].astype(BF16),
                            prev_ref[prows, 0:A_W].astype(BF16), prev_ref[prows, A_W:2 * A_W].astype(BF16), prev_ok)
        o_ref[rows, :] = o
        l_ref[rows, :] = le


def _attn_prompt(q3, kv3, dilation):
    rows = ATTN_QBLKS * ATTN_BLK
    nb = SEQ // dilation // rows
    qspec = pl.BlockSpec((None, rows, A_W), lambda b, r, n: (b, n, r))
    return pl.pallas_call(
        _attn_prompt_kernel,
        out_shape=[jax.ShapeDtypeStruct(q3.shape, F32)] * 2,
        grid=(BATCH, dilation, nb),
        in_specs=[qspec,
                  pl.BlockSpec((None, rows, 2 * A_W), lambda b, r, n: (b, n, r)),
                  pl.BlockSpec((None, ATTN_BLK, 2 * A_W),
                               lambda b, r, n: (b, jnp.maximum(ATTN_QBLKS * n - 1, 0), r))],
        out_specs=[qspec, qspec],
        name="attn_prompt",
    )(q3, kv3, kv3)


def _attn_sample_group(qpad, kvpad, cache_ref, window, dilation):
    cache_len = cache_ref.shape[1]
    n_rows = A_HEADS * SUBLANES
    tok = lax.broadcasted_iota(jnp.int32, (n_rows, 1), 0) & (SUBLANES - 1)
    delta = cache_len + tok - lax.broadcasted_iota(jnp.int32, (1, cache_len), 1)
    ok_c = ((delta & (dilation - 1)) == 0) & (delta >= dilation) & (delta <= window)
    npos = lax.broadcasted_iota(jnp.int32, (1, SUBLANES), 1)
    dnew = tok - npos
    ok_n = (dnew >= 0) & ((dnew & (dilation - 1)) == 0) & (dnew <= window) & (npos < DEC_SEQ)
    lane_head = lax.broadcasted_iota(jnp.int32, (SUBLANES, A_W), 1) // A_DH
    qs = qpad * (A_DH ** -0.5)
    qs = jnp.concatenate([jnp.where(lane_head == h, qs, 0.0) for h in range(A_HEADS)], axis=0).astype(BF16)
    kt = cache_ref[0:A_W, :].astype(BF16)
    vt = cache_ref[A_W:2 * A_W, :].astype(BF16)
    s_c = jnp.where(ok_c, jnp.dot(qs, kt, preferred_element_type=F32), NEG_INF)
    s_n = jnp.where(ok_n, _nt_dot(qs, kvpad[:, 0:A_W].astype(BF16)), NEG_INF)
    mx = jnp.maximum(jnp.max(s_c, axis=-1, keepdims=True), jnp.max(s_n, axis=-1, keepdims=True))
    p_c = jnp.exp(s_c - mx)
    p_n = jnp.exp(s_n - mx)
    l = jnp.sum(p_c, axis=-1, keepdims=True) + jnp.sum(p_n, axis=-1, keepdims=True)
    inv = 1.0 / l
    pv = _nt_dot((p_c * inv).astype(BF16), vt) \
        + jnp.dot((p_n * inv).astype(BF16), kvpad[:, A_W:2 * A_W].astype(BF16), preferred_element_type=F32)
    lse = mx + jnp.log(l)
    o = jnp.zeros((SUBLANES, A_W), F32)
    le = jnp.zeros((SUBLANES, A_W), F32)
    for h in range(A_HEADS):
        rows = slice(h * SUBLANES, (h + 1) * SUBLANES)
        o = jnp.where(lane_head == h, pv[rows, :], o)
        le = jnp.where(lane_head == h, lse[rows, :], le)
    return o, le


def _attn_sample_kernel(q0_ref, kv0_ref, c0_ref, q1_ref, kv1_ref, c1_ref, q2_ref, kv2_ref, c2_ref,
                        o0_ref, l0_ref, o1_ref, l1_ref, o2_ref, l2_ref, qpad, kvpad):
    ins = ((q0_ref, kv0_ref, c0_ref, o0_ref, l0_ref), (q1_ref, kv1_ref, c1_ref, o1_ref, l1_ref),
           (q2_ref, kv2_ref, c2_ref, o2_ref, l2_ref))
    qpad[...] = jnp.zeros((SUBLANES, A_W), F32)
    kvpad[...] = jnp.zeros((SUBLANES, 2 * A_W), F32)
    for s in range(ATTN_SAMPLE_PAR):
        for (window, dilation), (q_ref, kv_ref, c_ref, o_ref, l_ref) in zip(ATTN_GROUPS, ins):
            qpad[0:DEC_SEQ, :] = q_ref[s]
            kvpad[0:DEC_SEQ, :] = kv_ref[s]
            o, le = _attn_sample_group(qpad[...], kvpad[...], c_ref.at[s], window, dilation)
            o_ref[s] = o[0:DEC_SEQ, :]
            l_ref[s] = le[0:DEC_SEQ, :]


def _attn_sample(qs, kvs, caches):
    args, in_specs = [], []
    par = ATTN_SAMPLE_PAR
    tok = lambda width: pl.BlockSpec((par, DEC_SEQ, width), lambda b: (b, 0, 0))
    for q, kv, cache in zip(qs, kvs, caches):
        cache_len = cache.shape[1]
        c3 = jnp.transpose(cache, (0, 2, 3, 4, 1)).reshape(DEC_BATCH, 2 * A_W, cache_len)
        args += [q.reshape(DEC_BATCH, DEC_SEQ, A_W), kv.reshape(DEC_BATCH, DEC_SEQ, 2 * A_W), c3]
        in_specs += [tok(A_W), tok(2 * A_W), pl.BlockSpec((par, 2 * A_W, cache_len), lambda b: (b, 0, 0))]
    outs = pl.pallas_call(
        _attn_sample_kernel,
        out_shape=[jax.ShapeDtypeStruct((DEC_BATCH, DEC_SEQ, A_W), F32)] * (2 * N_GROUPS),
        grid=(DEC_BATCH // par,),
        in_specs=in_specs,
        out_specs=[tok(A_W)] * (2 * N_GROUPS),
        scratch_shapes=[pltpu.VMEM((SUBLANES, A_W), F32), pltpu.VMEM((SUBLANES, 2 * A_W), F32)],
        compiler_params=pltpu.CompilerParams(vmem_limit_bytes=VMEM_LIMIT),
        name="attn_sample",
    )(*args)
    return [a.reshape(DEC_BATCH * DEC_SEQ, A_W) for a in outs]


def _merge_kernel(hm_ref, o0_ref, l0_ref, o1_ref, l1_ref, o2_ref, l2_ref, g_ref, x_ref, g1_ref, sc2_ref, sh2_ref,
                  wbm_ref, wba_ref, wo_ref, lng_ref, lnb_ref, wr_ref, br_ref, cnt0_ref,
                  x1_ref, h2_ref, ei_ref, eg_ref, er_ref, cnt_ref, run_s, scr, *, dils):
    i = pl.program_id(0)

    @pl.when(i == 0)
    def _():
        run_s[...] = cnt0_ref[...]

    l0, l1, l2 = [_load_dilated(r, scr, d) for r, d in zip((l0_ref, l1_ref, l2_ref), dils)]
    o0, o1, o2 = [_load_dilated(r, scr, d) for r, d in zip((o0_ref, o1_ref, o2_ref), dils)]
    mx = jnp.maximum(jnp.maximum(l0, l1), l2)
    e0, e1, e2 = jnp.exp(l0 - mx), jnp.exp(l1 - mx), jnp.exp(l2 - mx)
    inv = 1.0 / (e0 + e1 + e2)
    attn = (e0 * inv) * o0 + (e1 * inv) * o1 + (e2 * inv) * o2
    br_m = jnp.dot(hm_ref[...].astype(BF16), wbm_ref[...], preferred_element_type=F32)
    br_a = jnp.dot(attn.astype(BF16), wba_ref[...], preferred_element_type=F32)
    merged = g_ref[:, 0:D_MODEL] * br_m + g_ref[:, D_MODEL:2 * D_MODEL] * br_a
    mix = jnp.dot(merged.astype(BF16), wo_ref[...], preferred_element_type=F32)
    x1 = _ln(DN_ALPHA * x_ref[...] + g1_ref[...] * mix) * lng_ref[...] + lnb_ref[...]
    x1_ref[...] = x1
    h2 = _ln(x1) * (1.0 + sc2_ref[...]) + sh2_ref[...]
    h2_ref[...] = h2

    lane = lax.broadcasted_iota(jnp.int32, (TOK_TILE, LANES), 1)
    lane_f = lane.astype(F32)
    logits = jnp.dot(h2.astype(BF16), wr_ref[...], preferred_element_type=F32) + br_ref[...]
    cur = jnp.where(lane < N_EXPERTS, logits, NEG_INF)
    vals, onehots = [], []
    idx_mat = jnp.zeros((TOK_TILE, LANES), F32)
    for k in range(TOP_K):
        m = jnp.max(cur, axis=-1, keepdims=True)
        sel = jnp.min(jnp.where(cur == m, lane_f, float(LANES)), axis=-1, keepdims=True)
        hit = lane_f == sel
        vals.append(m)
        onehots.append(hit)
        idx_mat = jnp.where(lane == k, sel, idx_mat)
        cur = jnp.where(hit, NEG_INF, cur)
    es = [jnp.exp(v - vals[0]) for v in vals]
    tot = es[0] + es[1] + es[2] + es[3]
    gate_mat = jnp.zeros((TOK_TILE, LANES), F32)
    for k in range(TOP_K):
        gate_mat = jnp.where(lane == k, es[k] / tot, gate_mat)

    cnt = jnp.zeros((TOK_TILE, LANES), F32)
    for hit in onehots:
        cnt = cnt + hit.astype(F32)
    ri = lax.broadcasted_iota(jnp.int32, (TOK_TILE, TOK_TILE), 0)
    ci = lax.broadcasted_iota(jnp.int32, (TOK_TILE, TOK_TILE), 1)
    before = (ci < ri).astype(BF16)
    base = run_s[0:1, :] + jnp.dot(before, cnt.astype(BF16), preferred_element_type=F32)
    rank_mat = jnp.zeros((TOK_TILE, LANES), F32)
    for k, hit in enumerate(onehots):
        rk = jnp.sum(jnp.where(hit, base, 0.0), axis=-1, keepdims=True)
        rank_mat = jnp.where(lane == k, rk, rank_mat)
    new_run = run_s[0:1, :] + jnp.sum(cnt, axis=0, keepdims=True)
    run_s[...] = jnp.broadcast_to(new_run, (SUBLANES, LANES))
    cnt_ref[...] = run_s[...]
    ei_ref[...] = idx_mat[:, 0:TOP_K].astype(jnp.int32)
    eg_ref[...] = gate_mat[:, 0:TOP_K]
    er_ref[...] = rank_mat[:, 0:TOP_K].astype(jnp.int32)


def _merge(hm, attn_o, attn_l, gates, x2d, mod, wts, cnt0, per_token, tiles_per_batch, dils):
    n = x2d.shape[0]
    tok = lambda width, d=1: pl.BlockSpec((TOK_TILE // d, d * width), lambda i: (i, 0))
    const = lambda a: pl.BlockSpec(a.shape, lambda i: (0,) * a.ndim)
    mods = [_mod_spec(per_token, tiles_per_batch, j) for j in (MOD_GATE1, MOD_SCALE2, MOD_SHIFT2)]
    attn_args, attn_specs = [], []
    for o, l, d in zip(attn_o, attn_l, dils):
        attn_args += [o.reshape(n // d, d * A_W), l.reshape(n // d, d * A_W)]
        attn_specs += [tok(A_W, d), tok(A_W, d)]
    return pl.pallas_call(
        functools.partial(_merge_kernel, dils=dils),
        out_shape=[jax.ShapeDtypeStruct((n, D_MODEL), F32), jax.ShapeDtypeStruct((n, D_MODEL), F32),
                   jax.ShapeDtypeStruct((n, TOP_K), jnp.int32), jax.ShapeDtypeStruct((n, TOP_K), F32),
                   jax.ShapeDtypeStruct((n, TOP_K), jnp.int32), jax.ShapeDtypeStruct((SUBLANES, LANES), F32)],
        grid=(n // TOK_TILE,),
        in_specs=[tok(MV_W)] + attn_specs + [tok(2 * D_MODEL), tok(D_MODEL)] + mods
                 + [const(w) for w in wts] + [const(cnt0)],
        out_specs=[tok(D_MODEL), tok(D_MODEL), tok(TOP_K), tok(TOP_K), tok(TOP_K),
                   pl.BlockSpec((SUBLANES, LANES), lambda i: (0, 0))],
        scratch_shapes=[pltpu.VMEM((SUBLANES, LANES), F32), pltpu.VMEM((A_W // LANES, TOK_TILE, LANES), F32)],
        compiler_params=pltpu.CompilerParams(dimension_semantics=("arbitrary",), vmem_limit_bytes=VMEM_LIMIT),
        name="merge_router",
    )(hm, *attn_args, gates, x2d, mod, mod, mod, *wts, cnt0)


def _row_copy(src_hbm, src_row, dst, dst_row, sem):
    return pltpu.make_async_copy(src_hbm.at[pl.ds(src_row, 1)], dst.at[pl.ds(dst_row, 1)], sem)


def _dispatch_kernel(pend_ref, dest_ref, h2p_ref, h2s_ref, xs_hbm, zeros, sem, *, n_prompt_tiles):
    i = pl.program_id(0)

    @pl.when(i == 0)
    def _():
        zeros[...] = jnp.zeros((MOE_TILE, D_MODEL), F32)
        n_blk = xs_hbm.shape[0] // MOE_TILE
        used_blk = pend_ref[N_EXPERTS - 1] // MOE_TILE

        def zero_block(start):
            return pltpu.make_async_copy(zeros, xs_hbm.at[pl.ds(pl.multiple_of(start, MOE_TILE), MOE_TILE)], sem)

        def last_block(e):
            return zero_block(jnp.maximum(pend_ref[e] - MOE_TILE, 0))

        def fill(e, carry):
            last_block(e).start()
            return carry

        def fill_done(e, carry):
            last_block(e).wait()
            return carry

        def tail(b, carry):
            zero_block(b * MOE_TILE).start()
            return carry

        def tail_done(b, carry):
            zero_block(b * MOE_TILE).wait()
            return carry

        lax.fori_loop(0, N_EXPERTS, fill, 0)
        lax.fori_loop(used_blk, n_blk, tail, 0)
        lax.fori_loop(0, N_EXPERTS, fill_done, 0)
        lax.fori_loop(used_blk, n_blk, tail_done, 0)

    def scatter(h2_ref):
        def issue(t, carry):
            for k in range(TOP_K):
                _row_copy(h2_ref, t, xs_hbm, dest_ref[t * TOP_K + k], sem).start(priority=k % 2)
            return carry

        def drain(j, carry):
            _row_copy(h2_ref, 0, xs_hbm, 0, sem).wait()
            return carry

        lax.fori_loop(0, TOK_TILE, issue, 0, unroll=ISSUE_UNROLL)
        lax.fori_loop(0, TOK_TILE * TOP_K, drain, 0, unroll=8)

    @pl.when(i < n_prompt_tiles)
    def _():
        scatter(h2p_ref)

    @pl.when(i >= n_prompt_tiles)
    def _():
        scatter(h2s_ref)


def _dispatch(pend, dest_flat, h2_p, h2_s, n_rows):
    tp, ts = h2_p.shape[0] // TOK_TILE, h2_s.shape[0] // TOK_TILE
    return pl.pallas_call(
        functools.partial(_dispatch_kernel, n_prompt_tiles=tp),
        out_shape=jax.ShapeDtypeStruct((n_rows, D_MODEL), F32),
        grid_spec=pltpu.PrefetchScalarGridSpec(
            num_scalar_prefetch=1, grid=(tp + ts,),
            in_specs=[pl.BlockSpec((TOK_TILE * TOP_K,), lambda i, pe: (i,), memory_space=pltpu.SMEM),
                      pl.BlockSpec((TOK_TILE, D_MODEL), lambda i, pe: (jnp.minimum(i, tp - 1), 0)),
                      pl.BlockSpec((TOK_TILE, D_MODEL), lambda i, pe: (jnp.maximum(i - tp, 0), 0))],
            out_specs=pl.BlockSpec(memory_space=pl.ANY),
            scratch_shapes=[pltpu.VMEM((MOE_TILE, D_MODEL), F32), pltpu.SemaphoreType.DMA(())]),
        compiler_params=pltpu.CompilerParams(dimension_semantics=("arbitrary",)),
        name="moe_dispatch",
    )(pend, dest_flat, h2_p, h2_s)


CAST_ROWS = 128


def _expert_kernel(be_ref, bv_ref, xs_ref, wu_ref, bu_ref, wd_ref, bd_ref, ys_ref, wu_bf, wd_bf):
    i = pl.program_id(0)

    @pl.when((i == 0) | (be_ref[i] != be_ref[jnp.maximum(i - 1, 0)]))
    def _():
        def cast(r, carry):
            rows = pl.ds(pl.multiple_of(r * CAST_ROWS, CAST_ROWS), CAST_ROWS)
            wu_bf[rows, :] = wu_ref[rows, :].astype(BF16)
            wd_bf[rows, :] = wd_ref[rows, :].astype(BF16)
            return carry

        lax.fori_loop(0, D_MODEL // CAST_ROWS, cast, 0)

    @pl.when(bv_ref[i] == 1)
    def _():
        hu = jnp.dot(xs_ref[...].astype(BF16), wu_bf[...], preferred_element_type=F32) + bu_ref[...]
        x_glu = jnp.minimum(hu[:, 0:D_FF], SWIGLU_LIMIT)
        x_lin = jnp.clip(hu[:, D_FF:2 * D_FF], -SWIGLU_LIMIT, SWIGLU_LIMIT)
        act = x_glu * _sigmoid(SWIGLU_ALPHA * x_glu) * (x_lin + 1.0)
        ys_ref[...] = jnp.dot(act.astype(BF16), wd_bf[...], preferred_element_type=F32) + bd_ref[...]

    @pl.when(bv_ref[i] == 0)
    def _():
        ys_ref[...] = jnp.zeros((MOE_TILE, D_MODEL), F32)


def _experts(blk_e, blk_valid, xs, w_up, b_up, w_down, b_down):
    assert D_FF == D_MODEL
    n_blk = blk_e.shape[0]
    return pl.pallas_call(
        _expert_kernel,
        out_shape=jax.ShapeDtypeStruct(xs.shape, F32),
        grid_spec=pltpu.PrefetchScalarGridSpec(
            num_scalar_prefetch=2,
            grid=(n_blk,),
            in_specs=[pl.BlockSpec((MOE_TILE, D_MODEL), lambda i, be, bv: (i * bv[i], 0)),
                      pl.BlockSpec((None, D_MODEL, 2 * D_FF), lambda i, be, bv: (be[i], 0, 0)),
                      pl.BlockSpec((None, 1, 2 * D_FF), lambda i, be, bv: (be[i], 0, 0)),
                      pl.BlockSpec((None, D_FF, D_MODEL), lambda i, be, bv: (be[i], 0, 0)),
                      pl.BlockSpec((None, 1, D_MODEL), lambda i, be, bv: (be[i], 0, 0))],
            out_specs=pl.BlockSpec((MOE_TILE, D_MODEL), lambda i, be, bv: (i, 0)),
            scratch_shapes=[pltpu.VMEM((D_MODEL, 2 * D_FF), BF16), pltpu.VMEM((D_FF, D_MODEL), BF16)]),
        compiler_params=pltpu.CompilerParams(dimension_semantics=("arbitrary",), vmem_limit_bytes=VMEM_LIMIT),
        name="moe_experts",
    )(blk_e, blk_valid, xs, w_up, b_up.reshape(N_EXPERTS, 1, -1), w_down, b_down.reshape(N_EXPERTS, 1, -1))


def _combine_kernel(dest_ref, ys_hbm, gate_ref, x1_ref, g2_ref, lng_ref, lnb_ref, out_ref, rows, sem):
    def issue(t, carry):
        for k in range(TOP_K):
            _row_copy(ys_hbm, dest_ref[t * TOP_K + k], rows.at[k], t, sem).start(priority=k % 2)
        return carry

    def drain(j, carry):
        _row_copy(ys_hbm, 0, rows.at[0], 0, sem).wait()
        return carry

    lax.fori_loop(0, TOK_TILE, issue, 0, unroll=ISSUE_UNROLL)
    lax.fori_loop(0, TOK_TILE * TOP_K, drain, 0, unroll=8)
    gate = gate_ref[...]
    f = gate[:, 0:1] * rows[0]
    for k in range(1, TOP_K):
        f = f + gate[:, k:k + 1] * rows[k]
    out_ref[...] = _ln(DN_ALPHA * x1_ref[...] + g2_ref[...] * f) * lng_ref[...] + lnb_ref[...]


def _combine(dest_flat, ys, gate, x1, mod, ln_g, ln_b, per_token, tiles_per_batch):
    n = x1.shape[0]
    tok = lambda width: pl.BlockSpec((TOK_TILE, width), lambda i: (i, 0))
    const = pl.BlockSpec((1, D_MODEL), lambda i: (0, 0))
    return pl.pallas_call(
        _combine_kernel,
        out_shape=jax.ShapeDtypeStruct((n, D_MODEL), F32),
        grid=(n // TOK_TILE,),
        in_specs=[pl.BlockSpec((TOK_TILE * TOP_K,), lambda i: (i,), memory_space=pltpu.SMEM),
                  pl.BlockSpec(memory_space=pl.ANY), tok(TOP_K), tok(D_MODEL),
                  _mod_spec(per_token, tiles_per_batch, MOD_GATE2), const, const],
        out_specs=tok(D_MODEL),
        scratch_shapes=[pltpu.VMEM((TOP_K, TOK_TILE, D_MODEL), F32), pltpu.SemaphoreType.DMA(())],
        compiler_params=pltpu.CompilerParams(dimension_semantics=("arbitrary",), vmem_limit_bytes=VMEM_LIMIT),
        name="moe_combine",
    )(dest_flat, ys, gate, x1, mod, ln_g.reshape(1, -1), ln_b.reshape(1, -1))


def _routing(e_idx, rank, counts):
    n_assign = e_idx.shape[0] * TOP_K
    n_blk = n_assign // MOE_TILE + N_EXPERTS
    padded = (counts + MOE_TILE - 1) // MOE_TILE * MOE_TILE
    pend = jnp.cumsum(padded)
    pstart = pend - padded
    dest = pstart[e_idx] + rank
    blk_row = jnp.arange(n_blk, dtype=jnp.int32) * MOE_TILE
    blk_e = jnp.minimum(jnp.sum((pend[None, :] <= blk_row[:, None]).astype(jnp.int32), axis=1), N_EXPERTS - 1)
    blk_valid = (blk_row < pend[-1]).astype(jnp.int32)
    return dest.reshape(-1).astype(jnp.int32), pend.astype(jnp.int32), blk_e, blk_valid, n_blk * MOE_TILE


def kernel(x_prompt, x_sample, c_prompt, c_sample, cache_kv_w128, cache_kv_w512, cache_kv_w2048,
           state_mlstm_C, state_mlstm_n, state_mlstm_m, state_qk_conv,
           w_ada, b_ada, w_in, b_in, conv_w, conv_b, mh_norm_w, w_br_mlstm, w_br_attn,
           w_merge, b_merge, w_o, ln1_g, ln1_b, w_router, b_router, w_up, b_up, w_down, b_down,
           ln2_g, ln2_b):
    assert DEPTH == 1 and SEQ % (ATTN_GROUPS[-1][0]) == 0 and SEQ % MLSTM_CHUNK == 0
    n_p, n_s = BATCH * SEQ, DEC_BATCH * DEC_SEQ
    tiles_per_batch = SEQ // TOK_TILE
    layer = 0

    wi, bi = w_in[layer], b_in[layer]
    pad_if = LANES - 2 * M_HEADS
    w1 = jnp.concatenate([wi[:, :OFF_V], wi[:, OFF_V:OFF_I], wi[:, OFF_O:OFF_A], wi[:, OFF_A:], w_merge[layer],
                          wi[:, OFF_I:OFF_O], jnp.zeros((D_MODEL, pad_if), F32)], axis=1).astype(BF16)
    b1 = jnp.concatenate([bi[:OFF_V], bi[OFF_V:OFF_I], bi[OFF_O:OFF_A], bi[OFF_A:], b_merge[layer],
                          bi[OFF_I:OFF_O], jnp.zeros((pad_if,), F32)]).reshape(1, -1)
    wr = jnp.pad(w_router[layer], ((0, 0), (0, LANES - N_EXPERTS))).astype(BF16)
    br = jnp.pad(b_router[layer], (0, LANES - N_EXPERTS)).reshape(1, -1)
    merge_w = (w_br_mlstm[layer].astype(BF16), w_br_attn[layer].astype(BF16), w_o[layer].astype(BF16),
               ln1_g[layer].reshape(1, -1), ln1_b[layer].reshape(1, -1), wr, br)

    mod = _ada(jnp.concatenate([c_prompt, c_sample], axis=0), w_ada[layer], b_ada[layer])
    mod_p = mod[:BATCH].reshape(BATCH, 1, 6 * D_MODEL)
    mod_s = jnp.repeat(mod[BATCH:], DEC_SEQ, axis=0)

    results = {}
    cnt = jnp.zeros((SUBLANES, LANES), F32)
    for name, x, mods, per_token in (("p", x_prompt, mod_p, False), ("s", x_sample, mod_s, True)):
        x2d = x.reshape(-1, D_MODEL)
        dils = tuple(d for _, d in ATTN_GROUPS) if name == "p" else (1,) * N_GROUPS
        qk, v, o, q0, kv0, q1, kv1, q2, kv2, gates, ifg = _inproj(x2d, mods, w1, b1, per_token,
                                                                    tiles_per_batch, dils)
        qs, kvs = (q0, q1, q2), (kv0, kv1, kv2)
        if name == "p":
            zeros = lambda *s: jnp.zeros(s, F32)
            hm, c_new, n_new, m_new, hist = _mlstm(
                qk, v, o, ifg, conv_w[layer], conv_b[layer], mh_norm_w[layer],
                zeros(BATCH, SUBLANES, QK_W), zeros(BATCH, M_HEADS, M_DQK, M_DV), zeros(BATCH, SUBLANES, M_DQK),
                zeros(BATCH, SUBLANES, LANES), n_seq=BATCH, n_chunks=SEQ // MLSTM_CHUNK, chunk=MLSTM_CHUNK,
                valid=MLSTM_CHUNK, n_par=1)
            attn, new_kv = [], []
            for q, kv, (w, dil) in zip(qs, kvs, ATTN_GROUPS):
                kv3 = kv.reshape(BATCH, SEQ // dil, dil * 2 * A_W)
                attn.append(_attn_prompt(q.reshape(BATCH, SEQ // dil, dil * A_W), kv3, dil))
                win = min(w, SEQ)
                new_kv.append(kv3[:, (SEQ - win) // dil:].reshape(BATCH, win, 2, A_HEADS, A_DH))
            attn_o, attn_l = [a[0] for a in attn], [a[1] for a in attn]
            nb = BATCH
        else:
            m0 = jnp.broadcast_to(state_mlstm_m[layer][:, :, None], (DEC_BATCH, M_HEADS, LANES))
            hm, c_new, n_new, m_new, hist = _mlstm(
                qk, v, o, ifg, conv_w[layer], conv_b[layer], mh_norm_w[layer],
                _pad_rows(state_qk_conv[layer], SUBLANES, True), state_mlstm_C[layer],
                _pad_rows(state_mlstm_n[layer], SUBLANES, False), _pad_rows(m0, SUBLANES, False),
                n_seq=DEC_BATCH, n_chunks=1, chunk=SUBLANES, valid=DEC_SEQ, n_par=MLSTM_SAMPLE_PAR)
            caches = (cache_kv_w128[layer], cache_kv_w512[layer], cache_kv_w2048[layer])
            outs = _attn_sample(qs, kvs, caches)
            attn_o, attn_l = outs[0::2], outs[1::2]
            new_kv = [kv.reshape(DEC_BATCH, DEC_SEQ, 2, A_HEADS, A_DH) for kv in kvs]
            nb = DEC_BATCH
        x1, h2, e_idx, e_gate, e_rank, cnt_new = _merge(hm, attn_o, attn_l, gates, x2d, mods,
                                                         merge_w, cnt, per_token, tiles_per_batch, dils)
        results[name] = dict(
            x1=x1, h2=h2, e_idx=e_idx, e_gate=e_gate, e_rank=e_rank, mod=mods, per_token=per_token,
            states=(new_kv[0][None], new_kv[1][None], new_kv[2][None], c_new[None],
                    n_new[:, :M_HEADS][None], m_new[:, :M_HEADS, 0][None],
                    hist[:, SUBLANES - (CONV_W - 1):][None]), shape=x.shape)
        cnt = cnt_new

    e_idx = jnp.concatenate([results["p"]["e_idx"], results["s"]["e_idx"]], axis=0)
    e_rank = jnp.concatenate([results["p"]["e_rank"], results["s"]["e_rank"]], axis=0)
    counts = cnt[0, :N_EXPERTS].astype(jnp.int32)
    dest, pend, blk_e, blk_valid, n_rows = _routing(e_idx, e_rank, counts)
    xs = _dispatch(pend, dest, results["p"]["h2"], results["s"]["h2"], n_rows)
    ys = _experts(blk_e, blk_valid, xs, w_up[layer], b_up[layer], w_down[layer], b_down[layer])
    ys_out = {}
    off = 0
    for name in ("p", "s"):
        r = results[name]
        n = r["x1"].shape[0]
        ys_out[name] = _combine(dest[off * TOP_K:(off + n) * TOP_K], ys, r["e_gate"], r["x1"],
                                r["mod"], ln2_g[layer], ln2_b[layer], r["per_token"], tiles_per_batch
                                ).reshape(r["shape"])
        off += n
    return (ys_out["p"], ys_out["s"]) + results["p"]["states"] + results["s"]["states"]
```

```python
import functools

import jax
import jax.numpy as jnp
from jax import lax
from jax.experimental import pallas as pl
from jax.experimental.pallas import tpu as pltpu

D_MODEL = 1024
BATCH = 2
SEQ = 8192
DEPTH = 1
DEC_BATCH = 128
DEC_SEQ = 4
PAST_LEN = 2048

M_HEADS = 4
M_DQK = D_MODEL // 8
M_DV = D_MODEL // 4
CONV_W = 4
ATTN_GROUPS = ((128, 1), (512, 4), (2048, 16))
A_HEADS = 4
A_DH = D_MODEL // 16
N_EXPERTS = 32
TOP_K = 4
D_FF = D_MODEL
SWIGLU_LIMIT = 7.0
SWIGLU_ALPHA = 1.702
DN_ALPHA = (2 * DEPTH) ** 0.25
LN_EPS = 1e-5

QK_W = 2 * M_HEADS * M_DQK
MV_W = M_HEADS * M_DV
A_W = A_HEADS * A_DH
OFF_V = QK_W
OFF_I = OFF_V + MV_W
OFF_F = OFF_I + M_HEADS
OFF_O = OFF_F + M_HEADS
OFF_A = OFF_O + MV_W
N_GROUPS = len(ATTN_GROUPS)
P_W = OFF_A + 3 * A_W * N_GROUPS

LANES = 128
SUBLANES = 8
ATTN_BLK = 128
MLSTM_CHUNK = 128
MLSTM_SAMPLE_PAR = 4
ATTN_SAMPLE_PAR = 4
ISSUE_UNROLL = 4
TOK_TILE = 256
MOE_TILE = 512
VMEM_LIMIT = 56 * 1024 * 1024

SEG_QK = 0
SEG_V = SEG_QK + QK_W
SEG_O = SEG_V + MV_W
SEG_A = SEG_O + MV_W
SEG_G = SEG_A + 3 * A_W * N_GROUPS
SEG_IF = SEG_G + 2 * D_MODEL
W1_COLS = SEG_IF + LANES

F32 = jnp.float32
BF16 = jnp.bfloat16
NEG_INF = float("-inf")


def _ln(x):
    mu = jnp.mean(x, axis=-1, keepdims=True)
    xc = x - mu
    var = jnp.mean(xc * xc, axis=-1, keepdims=True)
    return xc * lax.rsqrt(var + LN_EPS)


def _sigmoid(x):
    return 1.0 / (1.0 + jnp.exp(-x))


def _nt_dot(a, b, precision=None):
    return lax.dot_general(a, b, (((1,), (1,)), ((), ())), preferred_element_type=F32, precision=precision)


def _tn_dot(a, b):
    return lax.dot_general(a, b, (((0,), (0,)), ((), ())), preferred_element_type=F32)


def _ada_kernel(c_ref, w_ref, b_ref, o_ref):
    c = c_ref[...]
    s = c * _sigmoid(c)
    o_ref[...] = jnp.dot(s.astype(BF16), w_ref[...].astype(BF16), preferred_element_type=F32) + b_ref[...]


def _ada(c_all, w_ada, b_ada):
    r = c_all.shape[0]
    n_col = 6 * D_MODEL // D_MODEL
    return pl.pallas_call(
        _ada_kernel,
        out_shape=jax.ShapeDtypeStruct((r, 6 * D_MODEL), F32),
        grid=(n_col,),
        in_specs=[pl.BlockSpec((r, D_MODEL), lambda j: (0, 0)),
                  pl.BlockSpec((D_MODEL, D_MODEL), lambda j: (0, j)),
                  pl.BlockSpec((1, D_MODEL), lambda j: (0, j))],
        out_specs=pl.BlockSpec((r, D_MODEL), lambda j: (0, j)),
        name="ada_mod",
    )(c_all, w_ada, b_ada.reshape(1, -1))


def _store_dilated(out_ref, val, scr, dil):
    w = val.shape[1]
    if dil == 1:
        out_ref[...] = val
        return
    for c in range(w // LANES):
        scr[c] = val[:, c * LANES:(c + 1) * LANES]
    for r in range(dil):
        for c in range(w // LANES):
            out_ref[:, r * w + c * LANES:r * w + (c + 1) * LANES] = scr[c, pl.ds(r, TOK_TILE // dil, stride=dil), :]


def _load_dilated(in_ref, scr, dil):
    if dil == 1:
        return in_ref[...]
    w = in_ref.shape[1] // dil
    for r in range(dil):
        for c in range(w // LANES):
            scr[c, pl.ds(r, TOK_TILE // dil, stride=dil), :] = in_ref[:, r * w + c * LANES:r * w + (c + 1) * LANES]
    return jnp.concatenate([scr[c] for c in range(w // LANES)], axis=-1)


def _inproj_kernel(x_ref, sc_ref, sh_ref, w_ref, b_ref,
                   qk_ref, v_ref, o_ref, q0_ref, kv0_ref, q1_ref, kv1_ref, q2_ref, kv2_ref, g_ref, if_ref, scr,
                   *, dils):
    h = _ln(x_ref[...]) * (1.0 + sc_ref[...]) + sh_ref[...]
    hb = h.astype(BF16)

    def seg(a, width):
        return jnp.dot(hb, w_ref[:, a:a + width], preferred_element_type=F32) + b_ref[:, a:a + width]

    qk_ref[...] = seg(SEG_QK, QK_W)
    v_ref[...] = seg(SEG_V, MV_W)
    o_ref[...] = seg(SEG_O, MV_W)
    for gi, (q_ref, kv_ref) in enumerate(((q0_ref, kv0_ref), (q1_ref, kv1_ref), (q2_ref, kv2_ref))):
        base = SEG_A + 3 * A_W * gi
        _store_dilated(q_ref, seg(base, A_W), scr, dils[gi])
        _store_dilated(kv_ref, seg(base + A_W, 2 * A_W), scr, dils[gi])
    g_ref[...] = _sigmoid(seg(SEG_G, 2 * D_MODEL))
    if_ref[...] = seg(SEG_IF, LANES)


MOD_SHIFT1, MOD_SCALE1, MOD_GATE1, MOD_SHIFT2, MOD_SCALE2, MOD_GATE2 = range(6)


def _mod_spec(per_token, tiles_per_batch, j):
    if per_token:
        return pl.BlockSpec((TOK_TILE, D_MODEL), lambda i: (i, j))
    return pl.BlockSpec((None, 1, D_MODEL), lambda i: (i // tiles_per_batch, 0, j))


def _inproj(x2d, mod, w1, b1, per_token, tiles_per_batch, dils):
    n = x2d.shape[0]
    tok = lambda width, d=1: pl.BlockSpec((TOK_TILE // d, d * width), lambda i: (i, 0))
    shape = lambda width, d=1: jax.ShapeDtypeStruct((n // d, d * width), F32)
    layout = [(QK_W, 1), (MV_W, 1), (MV_W, 1)]
    for d in dils:
        layout += [(A_W, d), (2 * A_W, d)]
    layout += [(2 * D_MODEL, 1), (LANES, 1)]
    return pl.pallas_call(
        functools.partial(_inproj_kernel, dils=dils),
        out_shape=[shape(w, d) for w, d in layout],
        grid=(n // TOK_TILE,),
        in_specs=[tok(D_MODEL), _mod_spec(per_token, tiles_per_batch, MOD_SCALE1),
                  _mod_spec(per_token, tiles_per_batch, MOD_SHIFT1),
                  pl.BlockSpec((D_MODEL, W1_COLS), lambda i: (0, 0), pipeline_mode=pl.Buffered(1)),
                  pl.BlockSpec((1, W1_COLS), lambda i: (0, 0))],
        out_specs=[tok(w, d) for w, d in layout],
        scratch_shapes=[pltpu.VMEM((2 * A_W // LANES, TOK_TILE, LANES), F32)],
        compiler_params=pltpu.CompilerParams(vmem_limit_bytes=VMEM_LIMIT),
        name="inproj",
    )(x2d, mod, mod, w1, b1)


def _mlstm_kernel(qk_ref, v_ref, o_ref, if_ref, cw_ref, cb_ref, nw_ref, *rest, chunk, valid, n_par):
    for s in range(n_par):
        _mlstm_seq(qk_ref.at[s], v_ref.at[s], o_ref.at[s], if_ref.at[s], cw_ref, cb_ref, nw_ref,
                   *[r.at[s] for r in rest], chunk=chunk, valid=valid)


def _mlstm_seq(qk_ref, v_ref, o_ref, if_ref, cw_ref, cb_ref, nw_ref, hist0_ref, c0_ref, n0_ref, m0_ref,
               hm_ref, cout_ref, nout_ref, mout_ref, hout_ref,
               xbuf, vbuf, obuf, gbuf, c_s, n_s, m_s, *, chunk, valid):
    c_idx = pl.program_id(1)
    last = pl.num_programs(1) - 1

    @pl.when(c_idx == 0)
    def _():
        xbuf[0:SUBLANES, :] = hist0_ref[...]
        c_s[...] = c0_ref[...]
        n_s[...] = n0_ref[...]
        m_s[...] = m0_ref[...]

    if valid < chunk:
        xbuf[SUBLANES:SUBLANES + chunk, :] = jnp.zeros((chunk, QK_W), F32)
        vbuf[...] = jnp.zeros((chunk, MV_W), F32)
        obuf[...] = jnp.zeros((chunk, MV_W), F32)
        gbuf[...] = jnp.zeros((chunk, LANES), F32)
        xbuf[SUBLANES:SUBLANES + valid, :] = qk_ref[...]
        vbuf[0:valid, :] = v_ref[...]
        obuf[0:valid, :] = o_ref[...]
        gbuf[0:valid, :] = if_ref[...]
        v_all, o_all, g_raw = vbuf[...], obuf[...], gbuf[...]
    else:
        xbuf[SUBLANES:SUBLANES + chunk, :] = qk_ref[...]
        v_all, o_all, g_raw = v_ref[...], o_ref[...], if_ref[...]

    conv = cb_ref[...]
    for j in range(CONV_W):
        off = SUBLANES - (CONV_W - 1) + j
        conv = conv + cw_ref[j:j + 1, :] * xbuf[off:off + chunk, :]
    new_hist = xbuf[valid:valid + SUBLANES, :]
    xbuf[0:SUBLANES, :] = new_hist
    act = conv * _sigmoid(conv)
    if valid < chunk:
        rows = lax.broadcasted_iota(jnp.int32, (chunk, 1), 0)
        act = jnp.where(rows < valid, act, 0.0)

    lane = lax.broadcasted_iota(jnp.int32, (chunk, LANES), 1)
    is_f = (lane >= M_HEADS) & (lane < 2 * M_HEADS)
    lf = jnp.minimum(g_raw, 0.0) - jnp.log(1.0 + jnp.exp(-jnp.abs(g_raw)))
    g_lin = jnp.where(is_f, lf, g_raw)
    if valid < chunk:
        rows_l = lax.broadcasted_iota(jnp.int32, (chunk, LANES), 0)
        g_lin = jnp.where((rows_l >= valid) & is_f, 0.0, g_lin)
    ri = lax.broadcasted_iota(jnp.int32, (chunk, chunk), 0)
    ci = lax.broadcasted_iota(jnp.int32, (chunk, chunk), 1)
    causal = ci <= ri
    tri = causal.astype(F32)
    csum = jnp.dot(tri, g_lin, preferred_element_type=F32, precision=lax.Precision.HIGHEST)
    m_col = jnp.where(is_f, csum, g_lin)
    sel = (lax.broadcasted_iota(jnp.int32, (SUBLANES, LANES), 0)
           == lax.broadcasted_iota(jnp.int32, (SUBLANES, LANES), 1)).astype(F32)
    m_row = _nt_dot(sel, m_col, precision=lax.Precision.HIGHEST)
    if valid < chunk:
        tcol = lax.broadcasted_iota(jnp.int32, (chunk, 1), 0)
        trow = lax.broadcasted_iota(jnp.int32, (1, chunk), 1)

    for h in range(M_HEADS):
        q = act[:, h * M_DQK:(h + 1) * M_DQK]
        k = act[:, QK_W // 2 + h * M_DQK:QK_W // 2 + (h + 1) * M_DQK] * (M_DQK ** -0.5)
        v = v_all[:, h * M_DV:(h + 1) * M_DV]
        qb, kb, vb = q.astype(BF16), k.astype(BF16), v.astype(BF16)
        li_col = m_col[:, h:h + 1]
        b_col = m_col[:, M_HEADS + h:M_HEADS + h + 1]
        li_row = m_row[h:h + 1, :]
        b_row = m_row[M_HEADS + h:M_HEADS + h + 1, :]
        if valid < chunk:
            li_col = jnp.where(tcol < valid, li_col, NEG_INF)
            li_row = jnp.where(trow < valid, li_row, NEG_INF)
        m_prev = m_s[h:h + 1, 0:1]
        c_prev = c_s[h]
        n_prev = n_s[h:h + 1, :]

        dmat = jnp.where(causal, b_col - b_row + li_row, NEG_INF)
        inter = b_col + m_prev
        m_t = jnp.maximum(inter, jnp.max(dmat, axis=-1, keepdims=True))
        s_qk = _nt_dot(qb, kb)
        a = jnp.exp(dmat - m_t) * s_qk
        sc = jnp.exp(inter - m_t)
        num = sc * jnp.dot(qb, c_prev.astype(BF16), preferred_element_type=F32) \
            + jnp.dot(a.astype(BF16), vb, preferred_element_type=F32)
        den = sc * jnp.sum(q * n_prev, axis=-1, keepdims=True) + jnp.sum(a, axis=-1, keepdims=True)
        hh = num / jnp.maximum(jnp.abs(den), jnp.exp(-m_t))

        b_end = b_col[chunk - 1:chunk, :]
        g_col = b_end - b_col + li_col
        g_row = b_end - b_row + li_row
        m_new = jnp.maximum(b_end + m_prev, jnp.max(g_row, axis=-1, keepdims=True))
        w_col = jnp.exp(g_col - m_new)
        decay = jnp.exp(b_end + m_prev - m_new)
        c_s[h] = decay * c_prev + _tn_dot(kb, (w_col * v).astype(BF16))
        n_s[h:h + 1, :] = decay * n_prev + jnp.sum(w_col * k, axis=0, keepdims=True)
        m_s[h:h + 1, :] = jnp.broadcast_to(m_new, (1, LANES))

        y = _ln(hh) * nw_ref[:, h * M_DV:(h + 1) * M_DV] * _sigmoid(o_all[:, h * M_DV:(h + 1) * M_DV])
        hm_ref[:, h * M_DV:(h + 1) * M_DV] = y[0:valid, :]

    @pl.when(c_idx == last)
    def _():
        cout_ref[...] = c_s[...]
        nout_ref[...] = n_s[...]
        mout_ref[...] = m_s[...]
        hout_ref[...] = xbuf[0:SUBLANES, :]


def _mlstm(qk, v, o, ifg, conv_w, conv_b, norm_w, hist0, c0, n0, m0, *, n_seq, n_chunks, chunk, valid, n_par):
    seq_len = n_chunks * valid
    tok = lambda width: pl.BlockSpec((n_par, valid, width), lambda b, c: (b, c, 0))
    args = tuple(a.reshape(n_seq, seq_len, a.shape[-1]) for a in (qk, v, o, ifg))
    const2 = lambda r, w: pl.BlockSpec((r, w), lambda b, c: (0, 0))
    per_seq = lambda *dims: pl.BlockSpec((n_par,) + dims, lambda b, c: (b,) + (0,) * len(dims))
    par = lambda *dims: pltpu.VMEM((n_par,) + dims, F32)
    outs = pl.pallas_call(
        functools.partial(_mlstm_kernel, chunk=chunk, valid=valid, n_par=n_par),
        out_shape=[jax.ShapeDtypeStruct((n_seq, seq_len, MV_W), F32),
                   jax.ShapeDtypeStruct((n_seq, M_HEADS, M_DQK, M_DV), F32),
                   jax.ShapeDtypeStruct((n_seq, SUBLANES, M_DQK), F32),
                   jax.ShapeDtypeStruct((n_seq, SUBLANES, LANES), F32),
                   jax.ShapeDtypeStruct((n_seq, SUBLANES, QK_W), F32)],
        grid=(n_seq // n_par, n_chunks),
        in_specs=[tok(QK_W), tok(MV_W), tok(MV_W), tok(LANES),
                  const2(CONV_W, QK_W), const2(1, QK_W), const2(1, MV_W),
                  per_seq(SUBLANES, QK_W), per_seq(M_HEADS, M_DQK, M_DV), per_seq(SUBLANES, M_DQK),
                  per_seq(SUBLANES, LANES)],
        out_specs=[tok(MV_W), per_seq(M_HEADS, M_DQK, M_DV), per_seq(SUBLANES, M_DQK),
                   per_seq(SUBLANES, LANES), per_seq(SUBLANES, QK_W)],
        scratch_shapes=[par(chunk + 2 * SUBLANES, QK_W), par(chunk, MV_W), par(chunk, MV_W), par(chunk, LANES),
                        par(M_HEADS, M_DQK, M_DV), par(SUBLANES, M_DQK), par(SUBLANES, LANES)],
        compiler_params=pltpu.CompilerParams(dimension_semantics=("arbitrary", "arbitrary"),
                                             vmem_limit_bytes=VMEM_LIMIT),
        name="mlstm",
    )(*args, conv_w, conv_b.reshape(1, -1), norm_w.reshape(1, -1), hist0, c0, n0, m0)
    hm = outs[0].reshape(n_seq * seq_len, MV_W)
    return hm, outs[1], outs[2], outs[3], outs[4]


def _pad_rows(a, rows, at_end):
    pad = rows - a.shape[1]
    cfg = ((0, 0), (pad, 0), (0, 0)) if at_end else ((0, 0), (0, pad), (0, 0))
    return jnp.pad(a, cfg)


ATTN_QBLKS = 4


def _attn_block(q, kc, vc, kp, vp, prev_ok):
    qi = lax.broadcasted_iota(jnp.int32, (ATTN_BLK, ATTN_BLK), 0)
    ki = lax.broadcasted_iota(jnp.int32, (ATTN_BLK, ATTN_BLK), 1)
    mask_c = ki <= qi
    mask_p = (ki >= qi) & prev_ok
    lane_head = lax.broadcasted_iota(jnp.int32, (ATTN_BLK, A_W), 1) // A_DH
    q = q * (A_DH ** -0.5)
    qs = jnp.concatenate([jnp.where(lane_head == h, q, 0.0) for h in range(A_HEADS)], axis=0).astype(BF16)
    mask_c = jnp.concatenate([mask_c] * A_HEADS, axis=0)
    mask_p = jnp.concatenate([mask_p] * A_HEADS, axis=0)
    s_c = jnp.where(mask_c, _nt_dot(qs, kc), NEG_INF)
    s_p = jnp.where(mask_p, _nt_dot(qs, kp), NEG_INF)
    mx = jnp.max(jnp.maximum(s_c, s_p), axis=-1, keepdims=True)
    p_c = jnp.exp(s_c - mx)
    p_p = jnp.exp(s_p - mx)
    l = jnp.sum(p_c + p_p, axis=-1, keepdims=True)
    inv = 1.0 / l
    pv = jnp.dot((p_c * inv).astype(BF16), vc, preferred_element_type=F32) \
        + jnp.dot((p_p * inv).astype(BF16), vp, preferred_element_type=F32)
    lse = mx + jnp.log(l)
    o = jnp.zeros((ATTN_BLK, A_W), F32)
    le = jnp.zeros((ATTN_BLK, A_W), F32)
    for h in range(A_HEADS):
        rows = slice(h * ATTN_BLK, (h + 1) * ATTN_BLK)
        o = jnp.where(lane_head == h, pv[rows, :], o)
        le = jnp.where(lane_head == h, lse[rows, :], le)
    return o, le


def _attn_prompt_kernel(q_ref, kvc_ref, kvp_ref, o_ref, l_ref):
    n = pl.program_id(2)
    for j in range(ATTN_QBLKS):
        rows = slice(j * ATTN_BLK, (j + 1) * ATTN_BLK)
        if j == 0:
            prev_ref, prows, prev_ok = kvp_ref, slice(0, ATTN_BLK), n > 0
        else:
            prev_ref, prows, prev_ok = kvc_ref, slice((j - 1) * ATTN_BLK, j * ATTN_BLK), True
        o, le = _attn_block(q_ref[rows, :],
                            kvc_ref[rows, 0:A_W].astype(BF16), kvc_ref[rows, A_W:2 * A_W].astype(BF16),
                            prev_ref[prows, 0:A_W].astype(BF16), prev_ref[prows, A_W:2 * A_W].astype(BF16), prev_ok)
        o_ref[rows, :] = o
        l_ref[rows, :] = le


def _attn_prompt(q3, kv3, dilation):
    rows = ATTN_QBLKS * ATTN_BLK
    nb = SEQ // dilation // rows
    qspec = pl.BlockSpec((None, rows, A_W), lambda b, r, n: (b, n, r))
    return pl.pallas_call(
        _attn_prompt_kernel,
        out_shape=[jax.ShapeDtypeStruct(q3.shape, F32)] * 2,
        grid=(BATCH, dilation, nb),
        in_specs=[qspec,
                  pl.BlockSpec((None, rows, 2 * A_W), lambda b, r, n: (b, n, r)),
                  pl.BlockSpec((None, ATTN_BLK, 2 * A_W),
                               lambda b, r, n: (b, jnp.maximum(ATTN_QBLKS * n - 1, 0), r))],
        out_specs=[qspec, qspec],
        name="attn_prompt",
    )(q3, kv3, kv3)


def _attn_sample_group(qpad, kvpad, cache_ref, window, dilation):
    cache_len = cache_ref.shape[1]
    n_rows = A_HEADS * SUBLANES
    tok = lax.broadcasted_iota(jnp.int32, (n_rows, 1), 0) & (SUBLANES - 1)
    delta = cache_len + tok - lax.broadcasted_iota(jnp.int32, (1, cache_len), 1)
    ok_c = ((delta & (dilation - 1)) == 0) & (delta >= dilation) & (delta <= window)
    npos = lax.broadcasted_iota(jnp.int32, (1, SUBLANES), 1)
    dnew = tok - npos
    ok_n = (dnew >= 0) & ((dnew & (dilation - 1)) == 0) & (dnew <= window) & (npos < DEC_SEQ)
    lane_head = lax.broadcasted_iota(jnp.int32, (SUBLANES, A_W), 1) // A_DH
    qs = qpad * (A_DH ** -0.5)
    qs = jnp.concatenate([jnp.where(lane_head == h, qs, 0.0) for h in range(A_HEADS)], axis=0).astype(BF16)
    kt = cache_ref[0:A_W, :].astype(BF16)
    vt = cache_ref[A_W:2 * A_W, :].astype(BF16)
    s_c = jnp.where(ok_c, jnp.dot(qs, kt, preferred_element_type=F32), NEG_INF)
    s_n = jnp.where(ok_n, _nt_dot(qs, kvpad[:, 0:A_W].astype(BF16)), NEG_INF)
    mx = jnp.maximum(jnp.max(s_c, axis=-1, keepdims=True), jnp.max(s_n, axis=-1, keepdims=True))
    p_c = jnp.exp(s_c - mx)
    p_n = jnp.exp(s_n - mx)
    l = jnp.sum(p_c, axis=-1, keepdims=True) + jnp.sum(p_n, axis=-1, keepdims=True)
    inv = 1.0 / l
    pv = _nt_dot((p_c * inv).astype(BF16), vt) \
        + jnp.dot((p_n * inv).astype(BF16), kvpad[:, A_W:2 * A_W].astype(BF16), preferred_element_type=F32)
    lse = mx + jnp.log(l)
    o = jnp.zeros((SUBLANES, A_W), F32)
    le = jnp.zeros((SUBLANES, A_W), F32)
    for h in range(A_HEADS):
        rows = slice(h * SUBLANES, (h + 1) * SUBLANES)
        o = jnp.where(lane_head == h, pv[rows, :], o)
        le = jnp.where(lane_head == h, lse[rows, :], le)
    return o, le


def _attn_sample_kernel(q0_ref, kv0_ref, c0_ref, q1_ref, kv1_ref, c1_ref, q2_ref, kv2_ref, c2_ref,
                        o0_ref, l0_ref, o1_ref, l1_ref, o2_ref, l2_ref, qpad, kvpad):
    ins = ((q0_ref, kv0_ref, c0_ref, o0_ref, l0_ref), (q1_ref, kv1_ref, c1_ref, o1_ref, l1_ref),
           (q2_ref, kv2_ref, c2_ref, o2_ref, l2_ref))
    qpad[...] = jnp.zeros((SUBLANES, A_W), F32)
    kvpad[...] = jnp.zeros((SUBLANES, 2 * A_W), F32)
    for s in range(ATTN_SAMPLE_PAR):
        for (window, dilation), (q_ref, kv_ref, c_ref, o_ref, l_ref) in zip(ATTN_GROUPS, ins):
            qpad[0:DEC_SEQ, :] = q_ref[s]
            kvpad[0:DEC_SEQ, :] = kv_ref[s]
            o, le = _attn_sample_group(qpad[...], kvpad[...], c_ref.at[s], window, dilation)
            o_ref[s] = o[0:DEC_SEQ, :]
            l_ref[s] = le[0:DEC_SEQ, :]


def _attn_sample(qs, kvs, caches):
    args, in_specs = [], []
    par = ATTN_SAMPLE_PAR
    tok = lambda width: pl.BlockSpec((par, DEC_SEQ, width), lambda b: (b, 0, 0))
    for q, kv, cache in zip(qs, kvs, caches):
        cache_len = cache.shape[1]
        c3 = jnp.transpose(cache, (0, 2, 3, 4, 1)).reshape(DEC_BATCH, 2 * A_W, cache_len)
        args += [q.reshape(DEC_BATCH, DEC_SEQ, A_W), kv.reshape(DEC_BATCH, DEC_SEQ, 2 * A_W), c3]
        in_specs += [tok(A_W), tok(2 * A_W), pl.BlockSpec((par, 2 * A_W, cache_len), lambda b: (b, 0, 0))]
    outs = pl.pallas_call(
        _attn_sample_kernel,
        out_shape=[jax.ShapeDtypeStruct((DEC_BATCH, DEC_SEQ, A_W), F32)] * (2 * N_GROUPS),
        grid=(DEC_BATCH // par,),
        in_specs=in_specs,
        out_specs=[tok(A_W)] * (2 * N_GROUPS),
        scratch_shapes=[pltpu.VMEM((SUBLANES, A_W), F32), pltpu.VMEM((SUBLANES, 2 * A_W), F32)],
        compiler_params=pltpu.CompilerParams(vmem_limit_bytes=VMEM_LIMIT),
        name="attn_sample",
    )(*args)
    return [a.reshape(DEC_BATCH * DEC_SEQ, A_W) for a in outs]


def _merge_kernel(hm_ref, o0_ref, l0_ref, o1_ref, l1_ref, o2_ref, l2_ref, g_ref, x_ref, g1_ref, sc2_ref, sh2_ref,
                  wbm_ref, wba_ref, wo_ref, lng_ref, lnb_ref, wr_ref, br_ref, cnt0_ref,
                  x1_ref, h2_ref, ei_ref, eg_ref, er_ref, cnt_ref, run_s, scr, *, dils):
    i = pl.program_id(0)

    @pl.when(i == 0)
    def _():
        run_s[...] = cnt0_ref[...]

    l0, l1, l2 = [_load_dilated(r, scr, d) for r, d in zip((l0_ref, l1_ref, l2_ref), dils)]
    o0, o1, o2 = [_load_dilated(r, scr, d) for r, d in zip((o0_ref, o1_ref, o2_ref), dils)]
    mx = jnp.maximum(jnp.maximum(l0, l1), l2)
    e0, e1, e2 = jnp.exp(l0 - mx), jnp.exp(l1 - mx), jnp.exp(l2 - mx)
    inv = 1.0 / (e0 + e1 + e2)
    attn = (e0 * inv) * o0 + (e1 * inv) * o1 + (e2 * inv) * o2
    br_m = jnp.dot(hm_ref[...].astype(BF16), wbm_ref[...], preferred_element_type=F32)
    br_a = jnp.dot(attn.astype(BF16), wba_ref[...], preferred_element_type=F32)
    merged = g_ref[:, 0:D_MODEL] * br_m + g_ref[:, D_MODEL:2 * D_MODEL] * br_a
    mix = jnp.dot(merged.astype(BF16), wo_ref[...], preferred_element_type=F32)
    x1 = _ln(DN_ALPHA * x_ref[...] + g1_ref[...] * mix) * lng_ref[...] + lnb_ref[...]
    x1_ref[...] = x1
    h2 = _ln(x1) * (1.0 + sc2_ref[...]) + sh2_ref[...]
    h2_ref[...] = h2

    lane = lax.broadcasted_iota(jnp.int32, (TOK_TILE, LANES), 1)
    lane_f = lane.astype(F32)
    logits = jnp.dot(h2.astype(BF16), wr_ref[...], preferred_element_type=F32) + br_ref[...]
    cur = jnp.where(lane < N_EXPERTS, logits, NEG_INF)
    vals, onehots = [], []
    idx_mat = jnp.zeros((TOK_TILE, LANES), F32)
    for k in range(TOP_K):
        m = jnp.max(cur, axis=-1, keepdims=True)
        sel = jnp.min(jnp.where(cur == m, lane_f, float(LANES)), axis=-1, keepdims=True)
        hit = lane_f == sel
        vals.append(m)
        onehots.append(hit)
        idx_mat = jnp.where(lane == k, sel, idx_mat)
        cur = jnp.where(hit, NEG_INF, cur)
    es = [jnp.exp(v - vals[0]) for v in vals]
    tot = es[0] + es[1] + es[2] + es[3]
    gate_mat = jnp.zeros((TOK_TILE, LANES), F32)
    for k in range(TOP_K):
        gate_mat = jnp.where(lane == k, es[k] / tot, gate_mat)

    cnt = jnp.zeros((TOK_TILE, LANES), F32)
    for hit in onehots:
        cnt = cnt + hit.astype(F32)
    ri = lax.broadcasted_iota(jnp.int32, (TOK_TILE, TOK_TILE), 0)
    ci = lax.broadcasted_iota(jnp.int32, (TOK_TILE, TOK_TILE), 1)
    before = (ci < ri).astype(BF16)
    base = run_s[0:1, :] + jnp.dot(before, cnt.astype(BF16), preferred_element_type=F32)
    rank_mat = jnp.zeros((TOK_TILE, LANES), F32)
    for k, hit in enumerate(onehots):
        rk = jnp.sum(jnp.where(hit, base, 0.0), axis=-1, keepdims=True)
        rank_mat = jnp.where(lane == k, rk, rank_mat)
    new_run = run_s[0:1, :] + jnp.sum(cnt, axis=0, keepdims=True)
    run_s[...] = jnp.broadcast_to(new_run, (SUBLANES, LANES))
    cnt_ref[...] = run_s[...]
    ei_ref[...] = idx_mat[:, 0:TOP_K].astype(jnp.int32)
    eg_ref[...] = gate_mat[:, 0:TOP_K]
    er_ref[...] = rank_mat[:, 0:TOP_K].astype(jnp.int32)


def _merge(hm, attn_o, attn_l, gates, x2d, mod, wts, cnt0, per_token, tiles_per_batch, dils):
    n = x2d.shape[0]
    tok = lambda width, d=1: pl.BlockSpec((TOK_TILE // d, d * width), lambda i: (i, 0))
    const = lambda a: pl.BlockSpec(a.shape, lambda i: (0,) * a.ndim)
    mods = [_mod_spec(per_token, tiles_per_batch, j) for j in (MOD_GATE1, MOD_SCALE2, MOD_SHIFT2)]
    attn_args, attn_specs = [], []
    for o, l, d in zip(attn_o, attn_l, dils):
        attn_args += [o.reshape(n // d, d * A_W), l.reshape(n // d, d * A_W)]
        attn_specs += [tok(A_W, d), tok(A_W, d)]
    return pl.pallas_call(
        functools.partial(_merge_kernel, dils=dils),
        out_shape=[jax.ShapeDtypeStruct((n, D_MODEL), F32), jax.ShapeDtypeStruct((n, D_MODEL), F32),
                   jax.ShapeDtypeStruct((n, TOP_K), jnp.int32), jax.ShapeDtypeStruct((n, TOP_K), F32),
                   jax.ShapeDtypeStruct((n, TOP_K), jnp.int32), jax.ShapeDtypeStruct((SUBLANES, LANES), F32)],
        grid=(n // TOK_TILE,),
        in_specs=[tok(MV_W)] + attn_specs + [tok(2 * D_MODEL), tok(D_MODEL)] + mods
                 + [const(w) for w in wts] + [const(cnt0)],
        out_specs=[tok(D_MODEL), tok(D_MODEL), tok(TOP_K), tok(TOP_K), tok(TOP_K),
                   pl.BlockSpec((SUBLANES, LANES), lambda i: (0, 0))],
        scratch_shapes=[pltpu.VMEM((SUBLANES, LANES), F32), pltpu.VMEM((A_W // LANES, TOK_TILE, LANES), F32)],
        compiler_params=pltpu.CompilerParams(dimension_semantics=("arbitrary",), vmem_limit_bytes=VMEM_LIMIT),
        name="merge_router",
    )(hm, *attn_args, gates, x2d, mod, mod, mod, *wts, cnt0)


def _row_copy(src_hbm, src_row, dst, dst_row, sem):
    return pltpu.make_async_copy(src_hbm.at[pl.ds(src_row, 1)], dst.at[pl.ds(dst_row, 1)], sem)


def _dispatch_kernel(pend_ref, dest_ref, h2p_ref, h2s_ref, xs_hbm, zeros, sem, *, n_prompt_tiles):
    i = pl.program_id(0)

    @pl.when(i == 0)
    def _():
        zeros[...] = jnp.zeros((MOE_TILE, D_MODEL), F32)
        n_blk = xs_hbm.shape[0] // MOE_TILE
        used_blk = pend_ref[N_EXPERTS - 1] // MOE_TILE

        def zero_block(start):
            return pltpu.make_async_copy(zeros, xs_hbm.at[pl.ds(pl.multiple_of(start, MOE_TILE), MOE_TILE)], sem)

        def last_block(e):
            return zero_block(jnp.maximum(pend_ref[e] - MOE_TILE, 0))

        def fill(e, carry):
            last_block(e).start()
            return carry

        def fill_done(e, carry):
            last_block(e).wait()
            return carry

        def tail(b, carry):
            zero_block(b * MOE_TILE).start()
            return carry

        def tail_done(b, carry):
            zero_block(b * MOE_TILE).wait()
            return carry

        lax.fori_loop(0, N_EXPERTS, fill, 0)
        lax.fori_loop(used_blk, n_blk, tail, 0)
        lax.fori_loop(0, N_EXPERTS, fill_done, 0)
        lax.fori_loop(used_blk, n_blk, tail_done, 0)

    def scatter(h2_ref):
        def issue(t, carry):
            for k in range(TOP_K):
                _row_copy(h2_ref, t, xs_hbm, dest_ref[t * TOP_K + k], sem).start(priority=k % 2)
            return carry

        def drain(j, carry):
            _row_copy(h2_ref, 0, xs_hbm, 0, sem).wait()
            return carry

        lax.fori_loop(0, TOK_TILE, issue, 0, unroll=ISSUE_UNROLL)
        lax.fori_loop(0, TOK_TILE * TOP_K, drain, 0, unroll=8)

    @pl.when(i < n_prompt_tiles)
    def _():
        scatter(h2p_ref)

    @pl.when(i >= n_prompt_tiles)
    def _():
        scatter(h2s_ref)


def _dispatch(pend, dest_flat, h2_p, h2_s, n_rows):
    tp, ts = h2_p.shape[0] // TOK_TILE, h2_s.shape[0] // TOK_TILE
    return pl.pallas_call(
        functools.partial(_dispatch_kernel, n_prompt_tiles=tp),
        out_shape=jax.ShapeDtypeStruct((n_rows, D_MODEL), F32),
        grid_spec=pltpu.PrefetchScalarGridSpec(
            num_scalar_prefetch=1, grid=(tp + ts,),
            in_specs=[pl.BlockSpec((TOK_TILE * TOP_K,), lambda i, pe: (i,), memory_space=pltpu.SMEM),
                      pl.BlockSpec((TOK_TILE, D_MODEL), lambda i, pe: (jnp.minimum(i, tp - 1), 0)),
                      pl.BlockSpec((TOK_TILE, D_MODEL), lambda i, pe: (jnp.maximum(i - tp, 0), 0))],
            out_specs=pl.BlockSpec(memory_space=pl.ANY),
            scratch_shapes=[pltpu.VMEM((MOE_TILE, D_MODEL), F32), pltpu.SemaphoreType.DMA(())]),
        compiler_params=pltpu.CompilerParams(dimension_semantics=("arbitrary",)),
        name="moe_dispatch",
    )(pend, dest_flat, h2_p, h2_s)


CAST_ROWS = 128


def _expert_kernel(be_ref, bv_ref, xs_ref, wu_ref, bu_ref, wd_ref, bd_ref, ys_ref, wu_bf, wd_bf):
    i = pl.program_id(0)

    @pl.when((i == 0) | (be_ref[i] != be_ref[jnp.maximum(i - 1, 0)]))
    def _():
        def cast(r, carry):
            rows = pl.ds(pl.multiple_of(r * CAST_ROWS, CAST_ROWS), CAST_ROWS)
            wu_bf[rows, :] = wu_ref[rows, :].astype(BF16)
            wd_bf[rows, :] = wd_ref[rows, :].astype(BF16)
            return carry

        lax.fori_loop(0, D_MODEL // CAST_ROWS, cast, 0)

    @pl.when(bv_ref[i] == 1)
    def _():
        hu = jnp.dot(xs_ref[...].astype(BF16), wu_bf[...], preferred_element_type=F32) + bu_ref[...]
        x_glu = jnp.minimum(hu[:, 0:D_FF], SWIGLU_LIMIT)
        x_lin = jnp.clip(hu[:, D_FF:2 * D_FF], -SWIGLU_LIMIT, SWIGLU_LIMIT)
        act = x_glu * _sigmoid(SWIGLU_ALPHA * x_glu) * (x_lin + 1.0)
        ys_ref[...] = jnp.dot(act.astype(BF16), wd_bf[...], preferred_element_type=F32) + bd_ref[...]

    @pl.when(bv_ref[i] == 0)
    def _():
        ys_ref[...] = jnp.zeros((MOE_TILE, D_MODEL), F32)


def _experts(blk_e, blk_valid, xs, w_up, b_up, w_down, b_down):
    assert D_FF == D_MODEL
    n_blk = blk_e.shape[0]
    return pl.pallas_call(
        _expert_kernel,
        out_shape=jax.ShapeDtypeStruct(xs.shape, F32),
        grid_spec=pltpu.PrefetchScalarGridSpec(
            num_scalar_prefetch=2,
            grid=(n_blk,),
            in_specs=[pl.BlockSpec((MOE_TILE, D_MODEL), lambda i, be, bv: (i * bv[i], 0)),
                      pl.BlockSpec((None, D_MODEL, 2 * D_FF), lambda i, be, bv: (be[i], 0, 0)),
                      pl.BlockSpec((None, 1, 2 * D_FF), lambda i, be, bv: (be[i], 0, 0)),
                      pl.BlockSpec((None, D_FF, D_MODEL), lambda i, be, bv: (be[i], 0, 0)),
                      pl.BlockSpec((None, 1, D_MODEL), lambda i, be, bv: (be[i], 0, 0))],
            out_specs=pl.BlockSpec((MOE_TILE, D_MODEL), lambda i, be, bv: (i, 0)),
            scratch_shapes=[pltpu.VMEM((D_MODEL, 2 * D_FF), BF16), pltpu.VMEM((D_FF, D_MODEL), BF16)]),
        compiler_params=pltpu.CompilerParams(dimension_semantics=("arbitrary",), vmem_limit_bytes=VMEM_LIMIT),
        name="moe_experts",
    )(blk_e, blk_valid, xs, w_up, b_up.reshape(N_EXPERTS, 1, -1), w_down, b_down.reshape(N_EXPERTS, 1, -1))


def _combine_kernel(dest_ref, ys_hbm, gate_ref, x1_ref, g2_ref, lng_ref, lnb_ref, out_ref, rows, sem):
    def issue(t, carry):
        for k in range(TOP_K):
            _row_copy(ys_hbm, dest_ref[t * TOP_K + k], rows.at[k], t, sem).start(priority=k % 2)
        return carry

    def drain(j, carry):
        _row_copy(ys_hbm, 0, rows.at[0], 0, sem).wait()
        return carry

    lax.fori_loop(0, TOK_TILE, issue, 0, unroll=ISSUE_UNROLL)
    lax.fori_loop(0, TOK_TILE * TOP_K, drain, 0, unroll=8)
    gate = gate_ref[...]
    f = gate[:, 0:1] * rows[0]
    for k in range(1, TOP_K):
        f = f + gate[:, k:k + 1] * rows[k]
    out_ref[...] = _ln(DN_ALPHA * x1_ref[...] + g2_ref[...] * f) * lng_ref[...] + lnb_ref[...]


def _combine(dest_flat, ys, gate, x1, mod, ln_g, ln_b, per_token, tiles_per_batch):
    n = x1.shape[0]
    tok = lambda width: pl.BlockSpec((TOK_TILE, width), lambda i: (i, 0))
    const = pl.BlockSpec((1, D_MODEL), lambda i: (0, 0))
    return pl.pallas_call(
        _combine_kernel,
        out_shape=jax.ShapeDtypeStruct((n, D_MODEL), F32),
        grid=(n // TOK_TILE,),
        in_specs=[pl.BlockSpec((TOK_TILE * TOP_K,), lambda i: (i,), memory_space=pltpu.SMEM),
                  pl.BlockSpec(memory_space=pl.ANY), tok(TOP_K), tok(D_MODEL),
                  _mod_spec(per_token, tiles_per_batch, MOD_GATE2), const, const],
        out_specs=tok(D_MODEL),
        scratch_shapes=[pltpu.VMEM((TOP_K, TOK_TILE, D_MODEL), F32), pltpu.SemaphoreType.DMA(())],
        compiler_params=pltpu.CompilerParams(dimension_semantics=("arbitrary",), vmem_limit_bytes=VMEM_LIMIT),
        name="moe_combine",
    )(dest_flat, ys, gate, x1, mod, ln_g.reshape(1, -1), ln_b.reshape(1, -1))


def _routing(e_idx, rank, counts):
    n_assign = e_idx.shape[0] * TOP_K
    n_blk = n_assign // MOE_TILE + N_EXPERTS
    padded = (counts + MOE_TILE - 1) // MOE_TILE * MOE_TILE
    pend = jnp.cumsum(padded)
    pstart = pend - padded
    dest = pstart[e_idx] + rank
    blk_row = jnp.arange(n_blk, dtype=jnp.int32) * MOE_TILE
    blk_e = jnp.minimum(jnp.sum((pend[None, :] <= blk_row[:, None]).astype(jnp.int32), axis=1), N_EXPERTS - 1)
    blk_valid = (blk_row < pend[-1]).astype(jnp.int32)
    return dest.reshape(-1).astype(jnp.int32), pend.astype(jnp.int32), blk_e, blk_valid, n_blk * MOE_TILE


def kernel(x_prompt, x_sample, c_prompt, c_sample, cache_kv_w128, cache_kv_w512, cache_kv_w2048,
           state_mlstm_C, state_mlstm_n, state_mlstm_m, state_qk_conv,
           w_ada, b_ada, w_in, b_in, conv_w, conv_b, mh_norm_w, w_br_mlstm, w_br_attn,
           w_merge, b_merge, w_o, ln1_g, ln1_b, w_router, b_router, w_up, b_up, w_down, b_down,
           ln2_g, ln2_b):
    assert DEPTH == 1 and SEQ % (ATTN_GROUPS[-1][0]) == 0 and SEQ % MLSTM_CHUNK == 0
    n_p, n_s = BATCH * SEQ, DEC_BATCH * DEC_SEQ
    tiles_per_batch = SEQ // TOK_TILE
    layer = 0

    wi, bi = w_in[layer], b_in[layer]
    pad_if = LANES - 2 * M_HEADS
    w1 = jnp.concatenate([wi[:, :OFF_V], wi[:, OFF_V:OFF_I], wi[:, OFF_O:OFF_A], wi[:, OFF_A:], w_merge[layer],
                          wi[:, OFF_I:OFF_O], jnp.zeros((D_MODEL, pad_if), F32)], axis=1).astype(BF16)
    b1 = jnp.concatenate([bi[:OFF_V], bi[OFF_V:OFF_I], bi[OFF_O:OFF_A], bi[OFF_A:], b_merge[layer],
                          bi[OFF_I:OFF_O], jnp.zeros((pad_if,), F32)]).reshape(1, -1)
    wr = jnp.pad(w_router[layer], ((0, 0), (0, LANES - N_EXPERTS))).astype(BF16)
    br = jnp.pad(b_router[layer], (0, LANES - N_EXPERTS)).reshape(1, -1)
    merge_w = (w_br_mlstm[layer].astype(BF16), w_br_attn[layer].astype(BF16), w_o[layer].astype(BF16),
               ln1_g[layer].reshape(1, -1), ln1_b[layer].reshape(1, -1), wr, br)

    mod = _ada(jnp.concatenate([c_prompt, c_sample], axis=0), w_ada[layer], b_ada[layer])
    mod_p = mod[:BATCH].reshape(BATCH, 1, 6 * D_MODEL)
    mod_s = jnp.repeat(mod[BATCH:], DEC_SEQ, axis=0)

    results = {}
    cnt = jnp.zeros((SUBLANES, LANES), F32)
    for name, x, mods, per_token in (("p", x_prompt, mod_p, False), ("s", x_sample, mod_s, True)):
        x2d = x.reshape(-1, D_MODEL)
        dils = tuple(d for _, d in ATTN_GROUPS) if name == "p" else (1,) * N_GROUPS
        qk, v, o, q0, kv0, q1, kv1, q2, kv2, gates, ifg = _inproj(x2d, mods, w1, b1, per_token,
                                                                    tiles_per_batch, dils)
        qs, kvs = (q0, q1, q2), (kv0, kv1, kv2)
        if name == "p":
            zeros = lambda *s: jnp.zeros(s, F32)
            hm, c_new, n_new, m_new, hist = _mlstm(
                qk, v, o, ifg, conv_w[layer], conv_b[layer], mh_norm_w[layer],
                zeros(BATCH, SUBLANES, QK_W), zeros(BATCH, M_HEADS, M_DQK, M_DV), zeros(BATCH, SUBLANES, M_DQK),
                zeros(BATCH, SUBLANES, LANES), n_seq=BATCH, n_chunks=SEQ // MLSTM_CHUNK, chunk=MLSTM_CHUNK,
                valid=MLSTM_CHUNK, n_par=1)
            attn, new_kv = [], []
            for q, kv, (w, dil) in zip(qs, kvs, ATTN_GROUPS):
                kv3 = kv.reshape(BATCH, SEQ // dil, dil * 2 * A_W)
                attn.append(_attn_prompt(q.reshape(BATCH, SEQ // dil, dil * A_W), kv3, dil))
                win = min(w, SEQ)
                new_kv.append(kv3[:, (SEQ - win) // dil:].reshape(BATCH, win, 2, A_HEADS, A_DH))
            attn_o, attn_l = [a[0] for a in attn], [a[1] for a in attn]
            nb = BATCH
        else:
            m0 = jnp.broadcast_to(state_mlstm_m[layer][:, :, None], (DEC_BATCH, M_HEADS, LANES))
            hm, c_new, n_new, m_new, hist = _mlstm(
                qk, v, o, ifg, conv_w[layer], conv_b[layer], mh_norm_w[layer],
                _pad_rows(state_qk_conv[layer], SUBLANES, True), state_mlstm_C[layer],
                _pad_rows(state_mlstm_n[layer], SUBLANES, False), _pad_rows(m0, SUBLANES, False),
                n_seq=DEC_BATCH, n_chunks=1, chunk=SUBLANES, valid=DEC_SEQ, n_par=MLSTM_SAMPLE_PAR)
            caches = (cache_kv_w128[layer], cache_kv_w512[layer], cache_kv_w2048[layer])
            outs = _attn_sample(qs, kvs, caches)
            attn_o, attn_l = outs[0::2], outs[1::2]
            new_kv = [kv.reshape(DEC_BATCH, DEC_SEQ, 2, A_HEADS, A_DH) for kv in kvs]
            nb = DEC_BATCH
        x1, h2, e_idx, e_gate, e_rank, cnt_new = _merge(hm, attn_o, attn_l, gates, x2d, mods,
                                                         merge_w, cnt, per_token, tiles_per_batch, dils)
        results[name] = dict(
            x1=x1, h2=h2, e_idx=e_idx, e_gate=e_gate, e_rank=e_rank, mod=mods, per_token=per_token,
            states=(new_kv[0][None], new_kv[1][None], new_kv[2][None], c_new[None],
                    n_new[:, :M_HEADS][None], m_new[:, :M_HEADS, 0][None],
                    hist[:, SUBLANES - (CONV_W - 1):][None]), shape=x.shape)
        cnt = cnt_new

    e_idx = jnp.concatenate([results["p"]["e_idx"], results["s"]["e_idx"]], axis=0)
    e_rank = jnp.concatenate([results["p"]["e_rank"], results["s"]["e_rank"]], axis=0)
    counts = cnt[0, :N_EXPERTS].astype(jnp.int32)
    dest, pend, blk_e, blk_valid, n_rows = _routing(e_idx, e_rank, counts)
    xs = _dispatch(pend, dest, results["p"]["h2"], results["s"]["h2"], n_rows)
    ys = _experts(blk_e, blk_valid, xs, w_up[layer], b_up[layer], w_down[layer], b_down[layer])
    ys_out = {}
    off = 0
    for name in ("p", "s"):
        r = results[name]
        n = r["x1"].shape[0]
        ys_out[name] = _combine(dest[off * TOP_K:(off + n) * TOP_K], ys, r["e_gate"], r["x1"],
                                r["mod"], ln2_g[layer], ln2_b[layer], r["per_token"], tiles_per_batch
                                ).reshape(r["shape"])
        off += n
    return (ys_out["p"], ys_out["s"]) + results["p"]["states"] + results["s"]["states"]
```

```python
import functools

import jax
import jax.numpy as jnp
from jax import lax
from jax.experimental import pallas as pl
from jax.experimental.pallas import tpu as pltpu

D_MODEL = 1024
BATCH = 2
SEQ = 8192
DEPTH = 1
DEC_BATCH = 128
DEC_SEQ = 4
PAST_LEN = 2048

M_HEADS = 4
M_DQK = D_MODEL // 8
M_DV = D_MODEL // 4
CONV_W = 4
ATTN_GROUPS = ((128, 1), (512, 4), (2048, 16))
A_HEADS = 4
A_DH = D_MODEL // 16
N_EXPERTS = 32
TOP_K = 4
D_FF = D_MODEL
SWIGLU_LIMIT = 7.0
SWIGLU_ALPHA = 1.702
DN_ALPHA = (2 * DEPTH) ** 0.25
LN_EPS = 1e-5

QK_W = 2 * M_HEADS * M_DQK
MV_W = M_HEADS * M_DV
A_W = A_HEADS * A_DH
OFF_V = QK_W
OFF_I = OFF_V + MV_W
OFF_F = OFF_I + M_HEADS
OFF_O = OFF_F + M_HEADS
OFF_A = OFF_O + MV_W
N_GROUPS = len(ATTN_GROUPS)
P_W = OFF_A + 3 * A_W * N_GROUPS

LANES = 128
SUBLANES = 8
ATTN_BLK = 128
MLSTM_CHUNK = 128
MLSTM_SAMPLE_PAR = 4
ATTN_SAMPLE_PAR = 4
ISSUE_UNROLL = 4
TOK_TILE = 512
INPROJ_TILE = 256
MOE_TILE = 512
VMEM_LIMIT = 56 * 1024 * 1024

SEG_QK = 0
SEG_V = SEG_QK + QK_W
SEG_O = SEG_V + MV_W
SEG_A = SEG_O + MV_W
SEG_G = SEG_A + 3 * A_W * N_GROUPS
SEG_IF = SEG_G + 2 * D_MODEL
W1_COLS = SEG_IF + LANES

F32 = jnp.float32
BF16 = jnp.bfloat16
NEG_INF = float("-inf")


def _ln(x):
    mu = jnp.mean(x, axis=-1, keepdims=True)
    xc = x - mu
    var = jnp.mean(xc * xc, axis=-1, keepdims=True)
    return xc * lax.rsqrt(var + LN_EPS)


def _sigmoid(x):
    return 1.0 / (1.0 + jnp.exp(-x))


def _nt_dot(a, b, precision=None):
    return lax.dot_general(a, b, (((1,), (1,)), ((), ())), preferred_element_type=F32, precision=precision)


def _tn_dot(a, b):
    return lax.dot_general(a, b, (((0,), (0,)), ((), ())), preferred_element_type=F32)


def _ada_kernel(c_ref, w_ref, b_ref, o_ref):
    c = c_ref[...]
    s = c * _sigmoid(c)
    o_ref[...] = jnp.dot(s.astype(BF16), w_ref[...].astype(BF16), preferred_element_type=F32) + b_ref[...]


def _ada(c_all, w_ada, b_ada):
    r = c_all.shape[0]
    n_col = 6 * D_MODEL // D_MODEL
    return pl.pallas_call(
        _ada_kernel,
        out_shape=jax.ShapeDtypeStruct((r, 6 * D_MODEL), F32),
        grid=(n_col,),
        in_specs=[pl.BlockSpec((r, D_MODEL), lambda j: (0, 0)),
                  pl.BlockSpec((D_MODEL, D_MODEL), lambda j: (0, j)),
                  pl.BlockSpec((1, D_MODEL), lambda j: (0, j))],
        out_specs=pl.BlockSpec((r, D_MODEL), lambda j: (0, j)),
        name="ada_mod",
    )(c_all, w_ada, b_ada.reshape(1, -1))


def _store_dilated(out_ref, val, scr, dil):
    tile, w = val.shape
    if dil == 1:
        out_ref[...] = val
        return
    for c in range(w // LANES):
        scr[c] = val[:, c * LANES:(c + 1) * LANES]
    for r in range(dil):
        for c in range(w // LANES):
            out_ref[:, r * w + c * LANES:r * w + (c + 1) * LANES] = scr[c, pl.ds(r, tile // dil, stride=dil), :]


def _load_dilated(in_ref, scr, dil):
    if dil == 1:
        return in_ref[...]
    rows, w = in_ref.shape[0], in_ref.shape[1] // dil
    for r in range(dil):
        for c in range(w // LANES):
            scr[c, pl.ds(r, rows, stride=dil), :] = in_ref[:, r * w + c * LANES:r * w + (c + 1) * LANES]
    return jnp.concatenate([scr[c] for c in range(w // LANES)], axis=-1)


def _inproj_kernel(x_ref, sc_ref, sh_ref, w_ref, b_ref,
                   qk_ref, v_ref, o_ref, q0_ref, kv0_ref, q1_ref, kv1_ref, q2_ref, kv2_ref, g_ref, if_ref, scr,
                   *, dils):
    h = _ln(x_ref[...]) * (1.0 + sc_ref[...]) + sh_ref[...]
    hb = h.astype(BF16)

    def seg(a, width):
        return jnp.dot(hb, w_ref[:, a:a + width], preferred_element_type=F32) + b_ref[:, a:a + width]

    qk_ref[...] = seg(SEG_QK, QK_W)
    v_ref[...] = seg(SEG_V, MV_W)
    o_ref[...] = seg(SEG_O, MV_W)
    for gi, (q_ref, kv_ref) in enumerate(((q0_ref, kv0_ref), (q1_ref, kv1_ref), (q2_ref, kv2_ref))):
        base = SEG_A + 3 * A_W * gi
        _store_dilated(q_ref, seg(base, A_W), scr, dils[gi])
        _store_dilated(kv_ref, seg(base + A_W, 2 * A_W), scr, dils[gi])
    g_ref[...] = _sigmoid(seg(SEG_G, 2 * D_MODEL))
    if_ref[...] = seg(SEG_IF, LANES)


MOD_SHIFT1, MOD_SCALE1, MOD_GATE1, MOD_SHIFT2, MOD_SCALE2, MOD_GATE2 = range(6)


def _mod_spec(per_token, tile, j):
    if per_token:
        return pl.BlockSpec((tile, D_MODEL), lambda i: (i, j))
    tiles_per_batch = SEQ // tile
    return pl.BlockSpec((None, 1, D_MODEL), lambda i: (i // tiles_per_batch, 0, j))


def _inproj(x2d, mod, w1, b1, per_token, dils):
    n = x2d.shape[0]
    tok = lambda width, d=1: pl.BlockSpec((INPROJ_TILE // d, d * width), lambda i: (i, 0))
    shape = lambda width, d=1: jax.ShapeDtypeStruct((n // d, d * width), F32)
    layout = [(QK_W, 1), (MV_W, 1), (MV_W, 1)]
    for d in dils:
        layout += [(A_W, d), (2 * A_W, d)]
    layout += [(2 * D_MODEL, 1), (LANES, 1)]
    return pl.pallas_call(
        functools.partial(_inproj_kernel, dils=dils),
        out_shape=[shape(w, d) for w, d in layout],
        grid=(n // INPROJ_TILE,),
        in_specs=[tok(D_MODEL), _mod_spec(per_token, INPROJ_TILE, MOD_SCALE1),
                  _mod_spec(per_token, INPROJ_TILE, MOD_SHIFT1),
                  pl.BlockSpec((D_MODEL, W1_COLS), lambda i: (0, 0), pipeline_mode=pl.Buffered(1)),
                  pl.BlockSpec((1, W1_COLS), lambda i: (0, 0))],
        out_specs=[tok(w, d) for w, d in layout],
        scratch_shapes=[pltpu.VMEM((2 * A_W // LANES, INPROJ_TILE, LANES), F32)],
        compiler_params=pltpu.CompilerParams(vmem_limit_bytes=VMEM_LIMIT),
        name="inproj",
    )(x2d, mod, mod, w1, b1)


def _mlstm_kernel(qk_ref, v_ref, o_ref, if_ref, cw_ref, cb_ref, nw_ref, *rest, chunk, valid, n_par):
    for s in range(n_par):
        _mlstm_seq(qk_ref.at[s], v_ref.at[s], o_ref.at[s], if_ref.at[s], cw_ref, cb_ref, nw_ref,
                   *[r.at[s] for r in rest], chunk=chunk, valid=valid)


def _mlstm_seq(qk_ref, v_ref, o_ref, if_ref, cw_ref, cb_ref, nw_ref, hist0_ref, c0_ref, n0_ref, m0_ref,
               hm_ref, cout_ref, nout_ref, mout_ref, hout_ref,
               xbuf, vbuf, obuf, gbuf, c_s, n_s, m_s, *, chunk, valid):
    c_idx = pl.program_id(1)
    last = pl.num_programs(1) - 1

    @pl.when(c_idx == 0)
    def _():
        xbuf[0:SUBLANES, :] = hist0_ref[...]
        c_s[...] = c0_ref[...]
        n_s[...] = n0_ref[...]
        m_s[...] = m0_ref[...]

    if valid < chunk:
        xbuf[SUBLANES:SUBLANES + chunk, :] = jnp.zeros((chunk, QK_W), F32)
        vbuf[...] = jnp.zeros((chunk, MV_W), F32)
        obuf[...] = jnp.zeros((chunk, MV_W), F32)
        gbuf[...] = jnp.zeros((chunk, LANES), F32)
        xbuf[SUBLANES:SUBLANES + valid, :] = qk_ref[...]
        vbuf[0:valid, :] = v_ref[...]
        obuf[0:valid, :] = o_ref[...]
        gbuf[0:valid, :] = if_ref[...]
        v_all, o_all, g_raw = vbuf[...], obuf[...], gbuf[...]
    else:
        xbuf[SUBLANES:SUBLANES + chunk, :] = qk_ref[...]
        v_all, o_all, g_raw = v_ref[...], o_ref[...], if_ref[...]

    conv = cb_ref[...]
    for j in range(CONV_W):
        off = SUBLANES - (CONV_W - 1) + j
        conv = conv + cw_ref[j:j + 1, :] * xbuf[off:off + chunk, :]
    new_hist = xbuf[valid:valid + SUBLANES, :]
    xbuf[0:SUBLANES, :] = new_hist
    act = conv * _sigmoid(conv)
    if valid < chunk:
        rows = lax.broadcasted_iota(jnp.int32, (chunk, 1), 0)
        act = jnp.where(rows < valid, act, 0.0)

    lane = lax.broadcasted_iota(jnp.int32, (chunk, LANES), 1)
    is_f = (lane >= M_HEADS) & (lane < 2 * M_HEADS)
    lf = jnp.minimum(g_raw, 0.0) - jnp.log(1.0 + jnp.exp(-jnp.abs(g_raw)))
    g_lin = jnp.where(is_f, lf, g_raw)
    if valid < chunk:
        rows_l = lax.broadcasted_iota(jnp.int32, (chunk, LANES), 0)
        g_lin = jnp.where((rows_l >= valid) & is_f, 0.0, g_lin)
    ri = lax.broadcasted_iota(jnp.int32, (chunk, chunk), 0)
    ci = lax.broadcasted_iota(jnp.int32, (chunk, chunk), 1)
    causal = ci <= ri
    tri = causal.astype(F32)
    csum = jnp.dot(tri, g_lin, preferred_element_type=F32, precision=lax.Precision.HIGHEST)
    m_col = jnp.where(is_f, csum, g_lin)
    sel = (lax.broadcasted_iota(jnp.int32, (SUBLANES, LANES), 0)
           == lax.broadcasted_iota(jnp.int32, (SUBLANES, LANES), 1)).astype(F32)
    m_row = _nt_dot(sel, m_col, precision=lax.Precision.HIGHEST)
    if valid < chunk:
        tcol = lax.broadcasted_iota(jnp.int32, (chunk, 1), 0)
        trow = lax.broadcasted_iota(jnp.int32, (1, chunk), 1)

    for h in range(M_HEADS):
        q = act[:, h * M_DQK:(h + 1) * M_DQK]
        k = act[:, QK_W // 2 + h * M_DQK:QK_W // 2 + (h + 1) * M_DQK] * (M_DQK ** -0.5)
        v = v_all[:, h * M_DV:(h + 1) * M_DV]
        qb, kb, vb = q.astype(BF16), k.astype(BF16), v.astype(BF16)
        li_col = m_col[:, h:h + 1]
        b_col = m_col[:, M_HEADS + h:M_HEADS + h + 1]
        li_row = m_row[h:h + 1, :]
        b_row = m_row[M_HEADS + h:M_HEADS + h + 1, :]
        if valid < chunk:
            li_col = jnp.where(tcol < valid, li_col, NEG_INF)
            li_row = jnp.where(trow < valid, li_row, NEG_INF)
        m_prev = m_s[h:h + 1, 0:1]
        c_prev = c_s[h]
        n_prev = n_s[h:h + 1, :]

        dmat = jnp.where(causal, b_col - b_row + li_row, NEG_INF)
        inter = b_col + m_prev
        m_t = jnp.maximum(inter, jnp.max(dmat, axis=-1, keepdims=True))
        s_qk = _nt_dot(qb, kb)
        a = jnp.exp(dmat - m_t) * s_qk
        sc = jnp.exp(inter - m_t)
        num = sc * jnp.dot(qb, c_prev.astype(BF16), preferred_element_type=F32) \
            + jnp.dot(a.astype(BF16), vb, preferred_element_type=F32)
        den = sc * jnp.sum(q * n_prev, axis=-1, keepdims=True) + jnp.sum(a, axis=-1, keepdims=True)
        hh = num / jnp.maximum(jnp.abs(den), jnp.exp(-m_t))

        b_end = b_col[chunk - 1:chunk, :]
        g_col = b_end - b_col + li_col
        g_row = b_end - b_row + li_row
        m_new = jnp.maximum(b_end + m_prev, jnp.max(g_row, axis=-1, keepdims=True))
        w_col = jnp.exp(g_col - m_new)
        decay = jnp.exp(b_end + m_prev - m_new)
        c_s[h] = decay * c_prev + _tn_dot(kb, (w_col * v).astype(BF16))
        n_s[h:h + 1, :] = decay * n_prev + jnp.sum(w_col * k, axis=0, keepdims=True)
        m_s[h:h + 1, :] = jnp.broadcast_to(m_new, (1, LANES))

        y = _ln(hh) * nw_ref[:, h * M_DV:(h + 1) * M_DV] * _sigmoid(o_all[:, h * M_DV:(h + 1) * M_DV])
        hm_ref[:, h * M_DV:(h + 1) * M_DV] = y[0:valid, :]

    @pl.when(c_idx == last)
    def _():
        cout_ref[...] = c_s[...]
        nout_ref[...] = n_s[...]
        mout_ref[...] = m_s[...]
        hout_ref[...] = xbuf[0:SUBLANES, :]


def _mlstm(qk, v, o, ifg, conv_w, conv_b, norm_w, hist0, c0, n0, m0, *, n_seq, n_chunks, chunk, valid, n_par):
    seq_len = n_chunks * valid
    tok = lambda width: pl.BlockSpec((n_par, valid, width), lambda b, c: (b, c, 0))
    args = tuple(a.reshape(n_seq, seq_len, a.shape[-1]) for a in (qk, v, o, ifg))
    const2 = lambda r, w: pl.BlockSpec((r, w), lambda b, c: (0, 0))
    per_seq = lambda *dims: pl.BlockSpec((n_par,) + dims, lambda b, c: (b,) + (0,) * len(dims))
    par = lambda *dims: pltpu.VMEM((n_par,) + dims, F32)
    outs = pl.pallas_call(
        functools.partial(_mlstm_kernel, chunk=chunk, valid=valid, n_par=n_par),
        out_shape=[jax.ShapeDtypeStruct((n_seq, seq_len, MV_W), F32),
                   jax.ShapeDtypeStruct((n_seq, M_HEADS, M_DQK, M_DV), F32),
                   jax.ShapeDtypeStruct((n_seq, SUBLANES, M_DQK), F32),
                   jax.ShapeDtypeStruct((n_seq, SUBLANES, LANES), F32),
                   jax.ShapeDtypeStruct((n_seq, SUBLANES, QK_W), F32)],
        grid=(n_seq // n_par, n_chunks),
        in_specs=[tok(QK_W), tok(MV_W), tok(MV_W), tok(LANES),
                  const2(CONV_W, QK_W), const2(1, QK_W), const2(1, MV_W),
                  per_seq(SUBLANES, QK_W), per_seq(M_HEADS, M_DQK, M_DV), per_seq(SUBLANES, M_DQK),
                  per_seq(SUBLANES, LANES)],
        out_specs=[tok(MV_W), per_seq(M_HEADS, M_DQK, M_DV), per_seq(SUBLANES, M_DQK),
                   per_seq(SUBLANES, LANES), per_seq(SUBLANES, QK_W)],
        scratch_shapes=[par(chunk + 2 * SUBLANES, QK_W), par(chunk, MV_W), par(chunk, MV_W), par(chunk, LANES),
                        par(M_HEADS, M_DQK, M_DV), par(SUBLANES, M_DQK), par(SUBLANES, LANES)],
        compiler_params=pltpu.CompilerParams(dimension_semantics=("arbitrary", "arbitrary"),
                                             vmem_limit_bytes=VMEM_LIMIT),
        name="mlstm",
    )(*args, conv_w, conv_b.reshape(1, -1), norm_w.reshape(1, -1), hist0, c0, n0, m0)
    hm = outs[0].reshape(n_seq * seq_len, MV_W)
    return hm, outs[1], outs[2], outs[3], outs[4]


def _pad_rows(a, rows, at_end):
    pad = rows - a.shape[1]
    cfg = ((0, 0), (pad, 0), (0, 0)) if at_end else ((0, 0), (0, pad), (0, 0))
    return jnp.pad(a, cfg)


ATTN_QBLKS = 4


def _attn_block(q, kc, vc, kp, vp, prev_ok):
    qi = lax.broadcasted_iota(jnp.int32, (ATTN_BLK, ATTN_BLK), 0)
    ki = lax.broadcasted_iota(jnp.int32, (ATTN_BLK, ATTN_BLK), 1)
    mask_c = ki <= qi
    mask_p = (ki >= qi) & prev_ok
    lane_head = lax.broadcasted_iota(jnp.int32, (ATTN_BLK, A_W), 1) // A_DH
    q = q * (A_DH ** -0.5)
    qs = jnp.concatenate([jnp.where(lane_head == h, q, 0.0) for h in range(A_HEADS)], axis=0).astype(BF16)
    mask_c = jnp.concatenate([mask_c] * A_HEADS, axis=0)
    mask_p = jnp.concatenate([mask_p] * A_HEADS, axis=0)
    s_c = jnp.where(mask_c, _nt_dot(qs, kc), NEG_INF)
    s_p = jnp.where(mask_p, _nt_dot(qs, kp), NEG_INF)
    mx = jnp.max(jnp.maximum(s_c, s_p), axis=-1, keepdims=True)
    p_c = jnp.exp(s_c - mx)
    p_p = jnp.exp(s_p - mx)
    l = jnp.sum(p_c + p_p, axis=-1, keepdims=True)
    inv = 1.0 / l
    pv = jnp.dot((p_c * inv).astype(BF16), vc, preferred_element_type=F32) \
        + jnp.dot((p_p * inv).astype(BF16), vp, preferred_element_type=F32)
    lse = mx + jnp.log(l)
    o = jnp.zeros((ATTN_BLK, A_W), F32)
    le = jnp.zeros((ATTN_BLK, A_W), F32)
    for h in range(A_HEADS):
        rows = slice(h * ATTN_BLK, (h + 1) * ATTN_BLK)
        o = jnp.where(lane_head == h, pv[rows, :], o)
        le = jnp.where(lane_head == h, lse[rows, :], le)
    return o, le


def _attn_prompt_kernel(q_ref, kvc_ref, kvp_ref, o_ref, l_ref):
    n = pl.program_id(2)
    for j in range(ATTN_QBLKS):
        rows = slice(j * ATTN_BLK, (j + 1) * ATTN_BLK)
        if j == 0:
            prev_ref, prows, prev_ok = kvp_ref, slice(0, ATTN_BLK), n > 0
        else:
            prev_ref, prows, prev_ok = kvc_ref, slice((j - 1) * ATTN_BLK, j * ATTN_BLK), True
        o, le = _attn_block(q_ref[rows, :],
                            kvc_ref[rows, 0:A_W].astype(BF16), kvc_ref[rows, A_W:2 * A_W].astype(BF16),
                            prev_ref[prows, 0:A_W].astype(BF16), prev_ref[prows, A_W:2 * A_W].astype(BF16), prev_ok)
        o_ref[rows, :] = o
        l_ref[rows, :] = le


def _attn_prompt(q3, kv3, dilation):
    rows = ATTN_QBLKS * ATTN_BLK
    nb = SEQ // dilation // rows
    qspec = pl.BlockSpec((None, rows, A_W), lambda b, r, n: (b, n, r))
    return pl.pallas_call(
        _attn_prompt_kernel,
        out_shape=[jax.ShapeDtypeStruct(q3.shape, F32)] * 2,
        grid=(BATCH, dilation, nb),
        in_specs=[qspec,
                  pl.BlockSpec((None, rows, 2 * A_W), lambda b, r, n: (b, n, r)),
                  pl.BlockSpec((None, ATTN_BLK, 2 * A_W),
                               lambda b, r, n: (b, jnp.maximum(ATTN_QBLKS * n - 1, 0), r))],
        out_specs=[qspec, qspec],
        name="attn_prompt",
    )(q3, kv3, kv3)


def _attn_sample_group(qpad, kvpad, cache_ref, window, dilation):
    cache_len = cache_ref.shape[1]
    n_rows = A_HEADS * SUBLANES
    tok = lax.broadcasted_iota(jnp.int32, (n_rows, 1), 0) & (SUBLANES - 1)
    delta = cache_len + tok - lax.broadcasted_iota(jnp.int32, (1, cache_len), 1)
    ok_c = ((delta & (dilation - 1)) == 0) & (delta >= dilation) & (delta <= window)
    npos = lax.broadcasted_iota(jnp.int32, (1, SUBLANES), 1)
    dnew = tok - npos
    ok_n = (dnew >= 0) & ((dnew & (dilation - 1)) == 0) & (dnew <= window) & (npos < DEC_SEQ)
    lane_head = lax.broadcasted_iota(jnp.int32, (SUBLANES, A_W), 1) // A_DH
    qs = qpad * (A_DH ** -0.5)
    qs = jnp.concatenate([jnp.where(lane_head == h, qs, 0.0) for h in range(A_HEADS)], axis=0).astype(BF16)
    kt = cache_ref[0:A_W, :].astype(BF16)
    vt = cache_ref[A_W:2 * A_W, :].astype(BF16)
    s_c = jnp.where(ok_c, jnp.dot(qs, kt, preferred_element_type=F32), NEG_INF)
    s_n = jnp.where(ok_n, _nt_dot(qs, kvpad[:, 0:A_W].astype(BF16)), NEG_INF)
    mx = jnp.maximum(jnp.max(s_c, axis=-1, keepdims=True), jnp.max(s_n, axis=-1, keepdims=True))
    p_c = jnp.exp(s_c - mx)
    p_n = jnp.exp(s_n - mx)
    l = jnp.sum(p_c, axis=-1, keepdims=True) + jnp.sum(p_n, axis=-1, keepdims=True)
    inv = 1.0 / l
    pv = _nt_dot((p_c * inv).astype(BF16), vt) \
        + jnp.dot((p_n * inv).astype(BF16), kvpad[:, A_W:2 * A_W].astype(BF16), preferred_element_type=F32)
    lse = mx + jnp.log(l)
    o = jnp.zeros((SUBLANES, A_W), F32)
    le = jnp.zeros((SUBLANES, A_W), F32)
    for h in range(A_HEADS):
        rows = slice(h * SUBLANES, (h + 1) * SUBLANES)
        o = jnp.where(lane_head == h, pv[rows, :], o)
        le = jnp.where(lane_head == h, lse[rows, :], le)
    return o, le


def _attn_sample_kernel(q0_ref, kv0_ref, c0_ref, q1_ref, kv1_ref, c1_ref, q2_ref, kv2_ref, c2_ref,
                        o0_ref, l0_ref, o1_ref, l1_ref, o2_ref, l2_ref, qpad, kvpad):
    ins = ((q0_ref, kv0_ref, c0_ref, o0_ref, l0_ref), (q1_ref, kv1_ref, c1_ref, o1_ref, l1_ref),
           (q2_ref, kv2_ref, c2_ref, o2_ref, l2_ref))
    qpad[...] = jnp.zeros((SUBLANES, A_W), F32)
    kvpad[...] = jnp.zeros((SUBLANES, 2 * A_W), F32)
    for s in range(ATTN_SAMPLE_PAR):
        for (window, dilation), (q_ref, kv_ref, c_ref, o_ref, l_ref) in zip(ATTN_GROUPS, ins):
            qpad[0:DEC_SEQ, :] = q_ref[s]
            kvpad[0:DEC_SEQ, :] = kv_ref[s]
            o, le = _attn_sample_group(qpad[...], kvpad[...], c_ref.at[s], window, dilation)
            o_ref[s] = o[0:DEC_SEQ, :]
            l_ref[s] = le[0:DEC_SEQ, :]


def _attn_sample(qs, kvs, caches):
    args, in_specs = [], []
    par = ATTN_SAMPLE_PAR
    tok = lambda width: pl.BlockSpec((par, DEC_SEQ, width), lambda b: (b, 0, 0))
    for q, kv, cache in zip(qs, kvs, caches):
        cache_len = cache.shape[1]
        c3 = jnp.transpose(cache, (0, 2, 3, 4, 1)).reshape(DEC_BATCH, 2 * A_W, cache_len)
        args += [q.reshape(DEC_BATCH, DEC_SEQ, A_W), kv.reshape(DEC_BATCH, DEC_SEQ, 2 * A_W), c3]
        in_specs += [tok(A_W), tok(2 * A_W), pl.BlockSpec((par, 2 * A_W, cache_len), lambda b: (b, 0, 0))]
    outs = pl.pallas_call(
        _attn_sample_kernel,
        out_shape=[jax.ShapeDtypeStruct((DEC_BATCH, DEC_SEQ, A_W), F32)] * (2 * N_GROUPS),
        grid=(DEC_BATCH // par,),
        in_specs=in_specs,
        out_specs=[tok(A_W)] * (2 * N_GROUPS),
        scratch_shapes=[pltpu.VMEM((SUBLANES, A_W), F32), pltpu.VMEM((SUBLANES, 2 * A_W), F32)],
        compiler_params=pltpu.CompilerParams(vmem_limit_bytes=VMEM_LIMIT),
        name="attn_sample",
    )(*args)
    return [a.reshape(DEC_BATCH * DEC_SEQ, A_W) for a in outs]


def _merge_kernel(hm_ref, o0_ref, l0_ref, o1_ref, l1_ref, o2_ref, l2_ref, g_ref, x_ref, g1_ref, sc2_ref, sh2_ref,
                  wbm_ref, wba_ref, wo_ref, lng_ref, lnb_ref, wr_ref, br_ref, cnt0_ref,
                  x1_ref, h2_ref, ei_ref, eg_ref, er_ref, cnt_ref, run_s, scr, *, dils):
    i = pl.program_id(0)

    @pl.when(i == 0)
    def _():
        run_s[...] = cnt0_ref[...]

    l0, l1, l2 = [_load_dilated(r, scr, d) for r, d in zip((l0_ref, l1_ref, l2_ref), dils)]
    o0, o1, o2 = [_load_dilated(r, scr, d) for r, d in zip((o0_ref, o1_ref, o2_ref), dils)]
    mx = jnp.maximum(jnp.maximum(l0, l1), l2)
    e0, e1, e2 = jnp.exp(l0 - mx), jnp.exp(l1 - mx), jnp.exp(l2 - mx)
    inv = 1.0 / (e0 + e1 + e2)
    attn = (e0 * inv) * o0 + (e1 * inv) * o1 + (e2 * inv) * o2
    br_m = jnp.dot(hm_ref[...].astype(BF16), wbm_ref[...], preferred_element_type=F32)
    br_a = jnp.dot(attn.astype(BF16), wba_ref[...], preferred_element_type=F32)
    merged = g_ref[:, 0:D_MODEL] * br_m + g_ref[:, D_MODEL:2 * D_MODEL] * br_a
    mix = jnp.dot(merged.astype(BF16), wo_ref[...], preferred_element_type=F32)
    x1 = _ln(DN_ALPHA * x_ref[...] + g1_ref[...] * mix) * lng_ref[...] + lnb_ref[...]
    x1_ref[...] = x1
    h2 = _ln(x1) * (1.0 + sc2_ref[...]) + sh2_ref[...]
    h2_ref[...] = h2

    lane = lax.broadcasted_iota(jnp.int32, (TOK_TILE, LANES), 1)
    lane_f = lane.astype(F32)
    logits = jnp.dot(h2.astype(BF16), wr_ref[...], preferred_element_type=F32) + br_ref[...]
    cur = jnp.where(lane < N_EXPERTS, logits, NEG_INF)
    vals, onehots = [], []
    idx_mat = jnp.zeros((TOK_TILE, LANES), F32)
    for k in range(TOP_K):
        m = jnp.max(cur, axis=-1, keepdims=True)
        sel = jnp.min(jnp.where(cur == m, lane_f, float(LANES)), axis=-1, keepdims=True)
        hit = lane_f == sel
        vals.append(m)
        onehots.append(hit)
        idx_mat = jnp.where(lane == k, sel, idx_mat)
        cur = jnp.where(hit, NEG_INF, cur)
    es = [jnp.exp(v - vals[0]) for v in vals]
    tot = es[0] + es[1] + es[2] + es[3]
    gate_mat = jnp.zeros((TOK_TILE, LANES), F32)
    for k in range(TOP_K):
        gate_mat = jnp.where(lane == k, es[k] / tot, gate_mat)

    cnt = jnp.zeros((TOK_TILE, LANES), F32)
    for hit in onehots:
        cnt = cnt + hit.astype(F32)
    ri = lax.broadcasted_iota(jnp.int32, (TOK_TILE, TOK_TILE), 0)
    ci = lax.broadcasted_iota(jnp.int32, (TOK_TILE, TOK_TILE), 1)
    before = (ci < ri).astype(BF16)
    base = run_s[0:1, :] + jnp.dot(before, cnt.astype(BF16), preferred_element_type=F32)
    rank_mat = jnp.zeros((TOK_TILE, LANES), F32)
    for k, hit in enumerate(onehots):
        rk = jnp.sum(jnp.where(hit, base, 0.0), axis=-1, keepdims=True)
        rank_mat = jnp.where(lane == k, rk, rank_mat)
    new_run = run_s[0:1, :] + jnp.sum(cnt, axis=0, keepdims=True)
    run_s[...] = jnp.broadcast_to(new_run, (SUBLANES, LANES))
    cnt_ref[...] = run_s[...]
    ei_ref[...] = idx_mat[:, 0:TOP_K].astype(jnp.int32)
    eg_ref[...] = gate_mat[:, 0:TOP_K]
    er_ref[...] = rank_mat[:, 0:TOP_K].astype(jnp.int32)


def _merge(hm, attn_o, attn_l, gates, x2d, mod, wts, cnt0, per_token, dils):
    n = x2d.shape[0]
    tok = lambda width, d=1: pl.BlockSpec((TOK_TILE // d, d * width), lambda i: (i, 0))
    const = lambda a: pl.BlockSpec(a.shape, lambda i: (0,) * a.ndim)
    mods = [_mod_spec(per_token, TOK_TILE, j) for j in (MOD_GATE1, MOD_SCALE2, MOD_SHIFT2)]
    attn_args, attn_specs = [], []
    for o, l, d in zip(attn_o, attn_l, dils):
        attn_args += [o.reshape(n // d, d * A_W), l.reshape(n // d, d * A_W)]
        attn_specs += [tok(A_W, d), tok(A_W, d)]
    return pl.pallas_call(
        functools.partial(_merge_kernel, dils=dils),
        out_shape=[jax.ShapeDtypeStruct((n, D_MODEL), F32), jax.ShapeDtypeStruct((n, D_MODEL), F32),
                   jax.ShapeDtypeStruct((n, TOP_K), jnp.int32), jax.ShapeDtypeStruct((n, TOP_K), F32),
                   jax.ShapeDtypeStruct((n, TOP_K), jnp.int32), jax.ShapeDtypeStruct((SUBLANES, LANES), F32)],
        grid=(n // TOK_TILE,),
        in_specs=[tok(MV_W)] + attn_specs + [tok(2 * D_MODEL), tok(D_MODEL)] + mods
                 + [const(w) for w in wts] + [const(cnt0)],
        out_specs=[tok(D_MODEL), tok(D_MODEL), tok(TOP_K), tok(TOP_K), tok(TOP_K),
                   pl.BlockSpec((SUBLANES, LANES), lambda i: (0, 0))],
        scratch_shapes=[pltpu.VMEM((SUBLANES, LANES), F32), pltpu.VMEM((A_W // LANES, TOK_TILE, LANES), F32)],
        compiler_params=pltpu.CompilerParams(dimension_semantics=("arbitrary",), vmem_limit_bytes=VMEM_LIMIT),
        name="merge_router",
    )(hm, *attn_args, gates, x2d, mod, mod, mod, *wts, cnt0)


def _row_copy(src_hbm, src_row, dst, dst_row, sem):
    return pltpu.make_async_copy(src_hbm.at[pl.ds(src_row, 1)], dst.at[pl.ds(dst_row, 1)], sem)


def _dispatch_kernel(pend_ref, dest_ref, h2p_ref, h2s_ref, xs_hbm, zeros, sem, *, n_prompt_tiles):
    i = pl.program_id(0)

    @pl.when(i == 0)
    def _():
        zeros[...] = jnp.zeros((MOE_TILE, D_MODEL), F32)
        n_blk = xs_hbm.shape[0] // MOE_TILE
        used_blk = pend_ref[N_EXPERTS - 1] // MOE_TILE

        def zero_block(start):
            return pltpu.make_async_copy(zeros, xs_hbm.at[pl.ds(pl.multiple_of(start, MOE_TILE), MOE_TILE)], sem)

        def last_block(e):
            return zero_block(jnp.maximum(pend_ref[e] - MOE_TILE, 0))

        def fill(e, carry):
            last_block(e).start()
            return carry

        def fill_done(e, carry):
            last_block(e).wait()
            return carry

        def tail(b, carry):
            zero_block(b * MOE_TILE).start()
            return carry

        def tail_done(b, carry):
            zero_block(b * MOE_TILE).wait()
            return carry

        lax.fori_loop(0, N_EXPERTS, fill, 0)
        lax.fori_loop(used_blk, n_blk, tail, 0)
        lax.fori_loop(0, N_EXPERTS, fill_done, 0)
        lax.fori_loop(used_blk, n_blk, tail_done, 0)

    def scatter(h2_ref):
        def issue(t, carry):
            for k in range(TOP_K):
                _row_copy(h2_ref, t, xs_hbm, dest_ref[t * TOP_K + k], sem).start(priority=k % 2)
            return carry

        def drain(j, carry):
            _row_copy(h2_ref, 0, xs_hbm, 0, sem).wait()
            return carry

        lax.fori_loop(0, TOK_TILE, issue, 0, unroll=ISSUE_UNROLL)
        lax.fori_loop(0, TOK_TILE * TOP_K, drain, 0, unroll=8)

    @pl.when(i < n_prompt_tiles)
    def _():
        scatter(h2p_ref)

    @pl.when(i >= n_prompt_tiles)
    def _():
        scatter(h2s_ref)


def _dispatch(pend, dest_flat, h2_p, h2_s, n_rows):
    tp, ts = h2_p.shape[0] // TOK_TILE, h2_s.shape[0] // TOK_TILE
    return pl.pallas_call(
        functools.partial(_dispatch_kernel, n_prompt_tiles=tp),
        out_shape=jax.ShapeDtypeStruct((n_rows, D_MODEL), F32),
        grid_spec=pltpu.PrefetchScalarGridSpec(
            num_scalar_prefetch=1, grid=(tp + ts,),
            in_specs=[pl.BlockSpec((TOK_TILE * TOP_K,), lambda i, pe: (i,), memory_space=pltpu.SMEM),
                      pl.BlockSpec((TOK_TILE, D_MODEL), lambda i, pe: (jnp.minimum(i, tp - 1), 0)),
                      pl.BlockSpec((TOK_TILE, D_MODEL), lambda i, pe: (jnp.maximum(i - tp, 0), 0))],
            out_specs=pl.BlockSpec(memory_space=pl.ANY),
            scratch_shapes=[pltpu.VMEM((MOE_TILE, D_MODEL), F32), pltpu.SemaphoreType.DMA(())]),
        compiler_params=pltpu.CompilerParams(dimension_semantics=("arbitrary",)),
        name="moe_dispatch",
    )(pend, dest_flat, h2_p, h2_s)


CAST_ROWS = 128


def _expert_kernel(be_ref, bv_ref, xs_ref, wu_ref, bu_ref, wd_ref, bd_ref, ys_ref, wu_bf, wd_bf):
    i = pl.program_id(0)

    @pl.when((i == 0) | (be_ref[i] != be_ref[jnp.maximum(i - 1, 0)]))
    def _():
        def cast(r, carry):
            rows = pl.ds(pl.multiple_of(r * CAST_ROWS, CAST_ROWS), CAST_ROWS)
            wu_bf[rows, :] = wu_ref[rows, :].astype(BF16)
            wd_bf[rows, :] = wd_ref[rows, :].astype(BF16)
            return carry

        lax.fori_loop(0, D_MODEL // CAST_ROWS, cast, 0)

    @pl.when(bv_ref[i] == 1)
    def _():
        hu = jnp.dot(xs_ref[...].astype(BF16), wu_bf[...], preferred_element_type=F32) + bu_ref[...]
        x_glu = jnp.minimum(hu[:, 0:D_FF], SWIGLU_LIMIT)
        x_lin = jnp.clip(hu[:, D_FF:2 * D_FF], -SWIGLU_LIMIT, SWIGLU_LIMIT)
        act = x_glu * _sigmoid(SWIGLU_ALPHA * x_glu) * (x_lin + 1.0)
        ys_ref[...] = jnp.dot(act.astype(BF16), wd_bf[...], preferred_element_type=F32) + bd_ref[...]

    @pl.when(bv_ref[i] == 0)
    def _():
        ys_ref[...] = jnp.zeros((MOE_TILE, D_MODEL), F32)


def _experts(blk_e, blk_valid, xs, w_up, b_up, w_down, b_down):
    assert D_FF == D_MODEL
    n_blk = blk_e.shape[0]
    return pl.pallas_call(
        _expert_kernel,
        out_shape=jax.ShapeDtypeStruct(xs.shape, F32),
        grid_spec=pltpu.PrefetchScalarGridSpec(
            num_scalar_prefetch=2,
            grid=(n_blk,),
            in_specs=[pl.BlockSpec((MOE_TILE, D_MODEL), lambda i, be, bv: (i * bv[i], 0)),
                      pl.BlockSpec((None, D_MODEL, 2 * D_FF), lambda i, be, bv: (be[i], 0, 0)),
                      pl.BlockSpec((None, 1, 2 * D_FF), lambda i, be, bv: (be[i], 0, 0)),
                      pl.BlockSpec((None, D_FF, D_MODEL), lambda i, be, bv: (be[i], 0, 0)),
                      pl.BlockSpec((None, 1, D_MODEL), lambda i, be, bv: (be[i], 0, 0))],
            out_specs=pl.BlockSpec((MOE_TILE, D_MODEL), lambda i, be, bv: (i, 0)),
            scratch_shapes=[pltpu.VMEM((D_MODEL, 2 * D_FF), BF16), pltpu.VMEM((D_FF, D_MODEL), BF16)]),
        compiler_params=pltpu.CompilerParams(dimension_semantics=("arbitrary",), vmem_limit_bytes=VMEM_LIMIT),
        name="moe_experts",
    )(blk_e, blk_valid, xs, w_up, b_up.reshape(N_EXPERTS, 1, -1), w_down, b_down.reshape(N_EXPERTS, 1, -1))


def _combine_kernel(dest_ref, ys_hbm, gate_ref, x1_ref, g2_ref, lng_ref, lnb_ref, out_ref, rows, sem):
    def issue(t, carry):
        for k in range(TOP_K):
            _row_copy(ys_hbm, dest_ref[t * TOP_K + k], rows.at[k], t, sem).start(priority=k % 2)
        return carry

    def drain(j, carry):
        _row_copy(ys_hbm, 0, rows.at[0], 0, sem).wait()
        return carry

    lax.fori_loop(0, TOK_TILE, issue, 0, unroll=ISSUE_UNROLL)
    lax.fori_loop(0, TOK_TILE * TOP_K, drain, 0, unroll=8)
    gate = gate_ref[...]
    f = gate[:, 0:1] * rows[0]
    for k in range(1, TOP_K):
        f = f + gate[:, k:k + 1] * rows[k]
    out_ref[...] = _ln(DN_ALPHA * x1_ref[...] + g2_ref[...] * f) * lng_ref[...] + lnb_ref[...]


def _combine(dest_flat, ys, gate, x1, mod, ln_g, ln_b, per_token):
    n = x1.shape[0]
    tok = lambda width: pl.BlockSpec((TOK_TILE, width), lambda i: (i, 0))
    const = pl.BlockSpec((1, D_MODEL), lambda i: (0, 0))
    return pl.pallas_call(
        _combine_kernel,
        out_shape=jax.ShapeDtypeStruct((n, D_MODEL), F32),
        grid=(n // TOK_TILE,),
        in_specs=[pl.BlockSpec((TOK_TILE * TOP_K,), lambda i: (i,), memory_space=pltpu.SMEM),
                  pl.BlockSpec(memory_space=pl.ANY), tok(TOP_K), tok(D_MODEL),
                  _mod_spec(per_token, TOK_TILE, MOD_GATE2), const, const],
        out_specs=tok(D_MODEL),
        scratch_shapes=[pltpu.VMEM((TOP_K, TOK_TILE, D_MODEL), F32), pltpu.SemaphoreType.DMA(())],
        compiler_params=pltpu.CompilerParams(dimension_semantics=("arbitrary",), vmem_limit_bytes=VMEM_LIMIT),
        name="moe_combine",
    )(dest_flat, ys, gate, x1, mod, ln_g.reshape(1, -1), ln_b.reshape(1, -1))


def _routing(e_idx, rank, counts):
    n_assign = e_idx.shape[0] * TOP_K
    n_blk = n_assign // MOE_TILE + N_EXPERTS
    padded = (counts + MOE_TILE - 1) // MOE_TILE * MOE_TILE
    pend = jnp.cumsum(padded)
    pstart = pend - padded
    dest = pstart[e_idx] + rank
    blk_row = jnp.arange(n_blk, dtype=jnp.int32) * MOE_TILE
    blk_e = jnp.minimum(jnp.sum((pend[None, :] <= blk_row[:, None]).astype(jnp.int32), axis=1), N_EXPERTS - 1)
    blk_valid = (blk_row < pend[-1]).astype(jnp.int32)
    return dest.reshape(-1).astype(jnp.int32), pend.astype(jnp.int32), blk_e, blk_valid, n_blk * MOE_TILE


def kernel(x_prompt, x_sample, c_prompt, c_sample, cache_kv_w128, cache_kv_w512, cache_kv_w2048,
           state_mlstm_C, state_mlstm_n, state_mlstm_m, state_qk_conv,
           w_ada, b_ada, w_in, b_in, conv_w, conv_b, mh_norm_w, w_br_mlstm, w_br_attn,
           w_merge, b_merge, w_o, ln1_g, ln1_b, w_router, b_router, w_up, b_up, w_down, b_down,
           ln2_g, ln2_b):
    assert DEPTH == 1 and SEQ % (ATTN_GROUPS[-1][0]) == 0 and SEQ % MLSTM_CHUNK == 0
    layer = 0

    wi, bi = w_in[layer], b_in[layer]
    pad_if = LANES - 2 * M_HEADS
    w1 = jnp.concatenate([wi[:, :OFF_V], wi[:, OFF_V:OFF_I], wi[:, OFF_O:OFF_A], wi[:, OFF_A:], w_merge[layer],
                          wi[:, OFF_I:OFF_O], jnp.zeros((D_MODEL, pad_if), F32)], axis=1).astype(BF16)
    b1 = jnp.concatenate([bi[:OFF_V], bi[OFF_V:OFF_I], bi[OFF_O:OFF_A], bi[OFF_A:], b_merge[layer],
                          bi[OFF_I:OFF_O], jnp.zeros((pad_if,), F32)]).reshape(1, -1)
    wr = jnp.pad(w_router[layer], ((0, 0), (0, LANES - N_EXPERTS))).astype(BF16)
    br = jnp.pad(b_router[layer], (0, LANES - N_EXPERTS)).reshape(1, -1)
    merge_w = (w_br_mlstm[layer].astype(BF16), w_br_attn[layer].astype(BF16), w_o[layer].astype(BF16),
               ln1_g[layer].reshape(1, -1), ln1_b[layer].reshape(1, -1), wr, br)

    mod = _ada(jnp.concatenate([c_prompt, c_sample], axis=0), w_ada[layer], b_ada[layer])
    mod_p = mod[:BATCH].reshape(BATCH, 1, 6 * D_MODEL)
    mod_s = jnp.repeat(mod[BATCH:], DEC_SEQ, axis=0)

    results = {}
    cnt = jnp.zeros((SUBLANES, LANES), F32)
    for name, x, mods, per_token in (("p", x_prompt, mod_p, False), ("s", x_sample, mod_s, True)):
        x2d = x.reshape(-1, D_MODEL)
        dils = tuple(d for _, d in ATTN_GROUPS) if name == "p" else (1,) * N_GROUPS
        qk, v, o, q0, kv0, q1, kv1, q2, kv2, gates, ifg = _inproj(x2d, mods, w1, b1, per_token, dils)
        qs, kvs = (q0, q1, q2), (kv0, kv1, kv2)
        if name == "p":
            zeros = lambda *s: jnp.zeros(s, F32)
            hm, c_new, n_new, m_new, hist = _mlstm(
                qk, v, o, ifg, conv_w[layer], conv_b[layer], mh_norm_w[layer],
                zeros(BATCH, SUBLANES, QK_W), zeros(BATCH, M_HEADS, M_DQK, M_DV), zeros(BATCH, SUBLANES, M_DQK),
                zeros(BATCH, SUBLANES, LANES), n_seq=BATCH, n_chunks=SEQ // MLSTM_CHUNK, chunk=MLSTM_CHUNK,
                valid=MLSTM_CHUNK, n_par=1)
            attn, new_kv = [], []
            for q, kv, (w, dil) in zip(qs, kvs, ATTN_GROUPS):
                kv3 = kv.reshape(BATCH, SEQ // dil, dil * 2 * A_W)
                attn.append(_attn_prompt(q.reshape(BATCH, SEQ // dil, dil * A_W), kv3, dil))
                win = min(w, SEQ)
                new_kv.append(kv3[:, (SEQ - win) // dil:].reshape(BATCH, win, 2, A_HEADS, A_DH))
            attn_o, attn_l = [a[0] for a in attn], [a[1] for a in attn]
            nb = BATCH
        else:
            m0 = jnp.broadcast_to(state_mlstm_m[layer][:, :, None], (DEC_BATCH, M_HEADS, LANES))
            hm, c_new, n_new, m_new, hist = _mlstm(
                qk, v, o, ifg, conv_w[layer], conv_b[layer], mh_norm_w[layer],
                _pad_rows(state_qk_conv[layer], SUBLANES, True), state_mlstm_C[layer],
                _pad_rows(state_mlstm_n[layer], SUBLANES, False), _pad_rows(m0, SUBLANES, False),
                n_seq=DEC_BATCH, n_chunks=1, chunk=SUBLANES, valid=DEC_SEQ, n_par=MLSTM_SAMPLE_PAR)
            caches = (cache_kv_w128[layer], cache_kv_w512[layer], cache_kv_w2048[layer])
            outs = _attn_sample(qs, kvs, caches)
            attn_o, attn_l = outs[0::2], outs[1::2]
            new_kv = [kv.reshape(DEC_BATCH, DEC_SEQ, 2, A_HEADS, A_DH) for kv in kvs]
            nb = DEC_BATCH
        x1, h2, e_idx, e_gate, e_rank, cnt_new = _merge(hm, attn_o, attn_l, gates, x2d, mods,
                                                         merge_w, cnt, per_token, dils)
        results[name] = dict(
            x1=x1, h2=h2, e_idx=e_idx, e_gate=e_gate, e_rank=e_rank, mod=mods, per_token=per_token,
            states=(new_kv[0][None], new_kv[1][None], new_kv[2][None], c_new[None],
                    n_new[:, :M_HEADS][None], m_new[:, :M_HEADS, 0][None],
                    hist[:, SUBLANES - (CONV_W - 1):][None]), shape=x.shape)
        cnt = cnt_new

    e_idx = jnp.concatenate([results["p"]["e_idx"], results["s"]["e_idx"]], axis=0)
    e_rank = jnp.concatenate([results["p"]["e_rank"], results["s"]["e_rank"]], axis=0)
    counts = cnt[0, :N_EXPERTS].astype(jnp.int32)
    dest, pend, blk_e, blk_valid, n_rows = _routing(e_idx, e_rank, counts)
    xs = _dispatch(pend, dest, results["p"]["h2"], results["s"]["h2"], n_rows)
    ys = _experts(blk_e, blk_valid, xs, w_up[layer], b_up[layer], w_down[layer], b_down[layer])
    ys_out = {}
    off = 0
    for name in ("p", "s"):
        r = results[name]
        n = r["x1"].shape[0]
        ys_out[name] = _combine(dest[off * TOP_K:(off + n) * TOP_K], ys, r["e_gate"], r["x1"],
                                r["mod"], ln2_g[layer], ln2_b[layer], r["per_token"]).reshape(r["shape"])
        off += n
    return (ys_out["p"], ys_out["s"]) + results["p"]["states"] + results["s"]["states"]
```
